```python
import math
import jax, jax.numpy as jnp
from jax import lax
import numpy as np

D_MODEL = 4096
BATCH = 1
SEQ = 16384
DEPTH = 4

GRID_W = 64
CTX_LEN = 256
HEAD_DIM = 128
ROPE_THETA = 10000.0
NORM_EPS = 1e-6
BLOCK = 128
NEG = -1e30
N_MOD = 6

S5_WIDTH = D_MODEL // 4
S5_GROUP = 16
S5_GROUPS = S5_WIDTH // S5_GROUP
S5_STATE = 64
WINDOW = 128
B_HEADS = D_MODEL // (2 * HEAD_DIM)
B_KV_HEADS = B_HEADS // 4
C_HEADS = D_MODEL // HEAD_DIM
C_KV_HEADS = C_HEADS // 4
C_GROUP = C_HEADS // C_KV_HEADS
FFN_DENSE = 5632
N_EXPERTS = 8
TOP_K = 2
FFN_EXPERT = 896

N_EVEN = (DEPTH + 1) // 2
N_ODD = DEPTH // 2
AB_IN = S5_WIDTH + (B_HEADS + 2 * B_KV_HEADS) * HEAD_DIM
AB_OUT = S5_WIDTH + B_HEADS * HEAD_DIM
C_IN = (C_HEADS + 2 * C_KV_HEADS) * HEAD_DIM
C_OUT = C_HEADS * HEAD_DIM

kernel_name = "hybrid_s5_window_axial_moe_dit"


def rmsnorm(x, g):
    xf = x.astype(jnp.float32)
    y = xf * lax.rsqrt(jnp.mean(xf * xf, axis=-1, keepdims=True) + NORM_EPS)
    return (y * g.astype(jnp.float32)).astype(x.dtype)


def adaln(cond, w_mod, b_mod):
    m = jax.nn.silu(cond) @ w_mod + b_mod
    return m.reshape(cond.shape[0], N_MOD, -1)


def modulate(h, shift, scale):
    return h * (1.0 + scale[:, None, :]) + shift[:, None, :]


def axial_rope_tables(n_tokens):
    rows = n_tokens // GRID_W
    row_id = jnp.broadcast_to(jnp.arange(rows, dtype=jnp.float32)[:, None], (rows, GRID_W)).reshape(-1)
    col_id = jnp.broadcast_to(jnp.arange(GRID_W, dtype=jnp.float32)[None, :], (rows, GRID_W)).reshape(-1)
    n_freq = HEAD_DIM // 4
    inv = ROPE_THETA ** (-jnp.arange(n_freq, dtype=jnp.float32) / n_freq)
    ang_r = row_id[:, None] * inv[None, :]
    ang_c = col_id[:, None] * inv[None, :]
    return (jnp.cos(ang_r), jnp.sin(ang_r), jnp.cos(ang_c), jnp.sin(ang_c))


def _rot(u, cos, sin):
    a, b = jnp.split(u, 2, axis=-1)
    cos = cos[None, :, None, :]
    sin = sin[None, :, None, :]
    return jnp.concatenate([a * cos - b * sin, b * cos + a * sin], axis=-1)


def apply_axial_rope(x, rope):
    cos_r, sin_r, cos_c, sin_c = rope
    xf = x.astype(jnp.float32)
    x_row, x_col = jnp.split(xf, 2, axis=-1)
    out = jnp.concatenate([_rot(x_row, cos_r, sin_r), _rot(x_col, cos_c, sin_c)], axis=-1)
    return out.astype(x.dtype)


def gqa_softmax(q, k, v, sink=None, mask=None):
    n_kv, n_grp, hd = q.shape[2], q.shape[3], q.shape[4]
    s = jnp.einsum('bqkgd,bskd->bkgqs', q, k).astype(jnp.float32) * (hd ** -0.5)
    if mask is not None:
        s = jnp.where(mask, s, NEG)
    if sink is not None:
        sk = jnp.broadcast_to(sink.astype(jnp.float32).reshape(1, n_kv, n_grp, 1, 1), s.shape[:-1] + (1,))
        s = jnp.concatenate([s, sk], axis=-1)
    p = jax.nn.softmax(s, axis=-1)
    if sink is not None:
        p = p[..., :-1]
    return jnp.einsum('bkgqs,bskd->bqkgd', p.astype(v.dtype), v)


def _linear_recurrence(e1, e2):
    a1, b1 = e1
    a2, b2 = e2
    return a2 * a1, a2 * b1 + b2


def s5_bidirectional(u_c, u_l, lam_re, lam_im, log_dt, b_re, b_im, c_re, c_im, d_skip, ctx_out):
    f32 = jnp.float32
    u_c = u_c.astype(f32)
    u_l = u_l.astype(f32)
    d = d_skip.astype(f32)
    y_l = d * u_l
    y_c = d * u_c if ctx_out else None
    for direction, reverse in ((0, False), (1, True)):
        lam = lax.complex(lam_re[direction].astype(f32), lam_im[direction].astype(f32))
        dt = jnp.exp(log_dt[direction].astype(f32))[:, None]
        lam_bar = jnp.exp(lam * dt)
        b_bar = ((lam_bar - 1.0) / lam)[:, :, None] * lax.complex(b_re[direction].astype(f32), b_im[direction].astype(f32))
        c_mat = lax.complex(c_re[direction].astype(f32), c_im[direction].astype(f32))

        def scan(u, s0):
            bu = jnp.einsum('gph,btgh->tbgp', b_bar, u)
            if s0 is not None:
                edge = bu.shape[0] - 1 if reverse else 0
                bu = bu.at[edge].add(lam_bar * s0)
            a = jnp.broadcast_to(lam_bar, bu.shape)
            _, states = lax.associative_scan(_linear_recurrence, (a, bu), reverse=reverse, axis=0)
            return states

        st_c = scan(u_c, None)
        s_final = st_c[0] if reverse else st_c[-1]
        st_l = scan(u_l, s_final)
        y_l = y_l + jnp.einsum('ghp,tbgp->btgh', c_mat, st_l).real
        if ctx_out:
            y_c = y_c + jnp.einsum('ghp,tbgp->btgh', c_mat, st_c).real
    return y_c, y_l


def s5_glu(y, glu_w, glu_b):
    bsz, t = y.shape[0], y.shape[1]
    y = jax.nn.gelu(y.reshape(bsz, t, -1).astype(glu_w.dtype))
    return y * jax.nn.sigmoid(y @ glu_w + glu_b)


def windowed_latent_attention(q_l, k_l, v_l, k_c, v_c, sink):
    bsz, n_lat, n_heads, hd = q_l.shape
    n_kv = k_l.shape[2]
    nb = n_lat // BLOCK
    n_ctx = k_c.shape[1]
    pad = ((0, 0), (BLOCK, BLOCK), (0, 0), (0, 0))

    def windows(t):
        tp = jnp.pad(t, pad).reshape(bsz, nb + 2, BLOCK, n_kv, hd)
        w = jnp.concatenate([tp[:, :-2], tp[:, 1:-1], tp[:, 2:]], axis=2)
        return jnp.moveaxis(w, 1, 0)

    k_win = windows(k_l)
    v_win = windows(v_l)
    q_blk = jnp.moveaxis(q_l.reshape(bsz, nb, BLOCK, n_kv, n_heads // n_kv, hd), 1, 0)
    q_off = jnp.arange(BLOCK)
    k_off = jnp.arange(3 * BLOCK) - BLOCK
    band = jnp.abs(q_off[:, None] - k_off[None, :]) <= WINDOW
    ctx_cols = jnp.ones((BLOCK, n_ctx), dtype=bool)

    def one_block(args):
        n, qn, kn, vn = args
        kpos = n * BLOCK + k_off
        inside = (kpos >= 0) & (kpos < n_lat)
        mask = jnp.concatenate([band & inside[None, :], ctx_cols], axis=1)
        return gqa_softmax(qn, jnp.concatenate([kn, k_c], axis=1), jnp.concatenate([vn, v_c], axis=1), sink=sink, mask=mask)

    o = lax.map(one_block, (jnp.arange(nb), q_blk, k_win, v_win))
    return jnp.moveaxis(o, 0, 1).reshape(bsz, n_lat, n_heads * hd)


def mixer_ab(hc, hl, w_in, w_out, lam_re, lam_im, log_dt, b_re, b_im, c_re, c_im, d_skip, glu_w, glu_b, sink, rope, ctx_out):
    bsz, n_ctx, _ = hc.shape
    proj = jnp.concatenate([hc, hl], axis=1) @ w_in
    o1 = S5_WIDTH
    o2 = o1 + B_HEADS * HEAD_DIM
    o3 = o2 + B_KV_HEADS * HEAD_DIM

    def split(p):
        t = p.shape[1]
        u = p[..., :o1].reshape(bsz, t, S5_GROUPS, S5_GROUP)
        q = p[..., o1:o2].reshape(bsz, t, B_HEADS, HEAD_DIM)
        k = p[..., o2:o3].reshape(bsz, t, B_KV_HEADS, HEAD_DIM)
        v = p[..., o3:].reshape(bsz, t, B_KV_HEADS, HEAD_DIM)
        return u, q, k, v

    u_c, q_c, k_c, v_c = split(proj[:, :n_ctx])
    u_l, q_l, k_l, v_l = split(proj[:, n_ctx:])
    q_l = apply_axial_rope(q_l, rope)
    k_l = apply_axial_rope(k_l, rope)
    s_c, s_l = s5_bidirectional(u_c, u_l, lam_re, lam_im, log_dt, b_re, b_im, c_re, c_im, d_skip, ctx_out)
    a_l = s5_glu(s_l, glu_w, glu_b)
    o_l = windowed_latent_attention(q_l, k_l, v_l, k_c, v_c, sink)
    y_l = jnp.concatenate([a_l.astype(o_l.dtype), o_l], axis=-1) @ w_out
    y_c = None
    if ctx_out:
        a_c = s5_glu(s_c, glu_w, glu_b)
        qg = q_c.reshape(bsz, n_ctx, B_KV_HEADS, B_HEADS // B_KV_HEADS, HEAD_DIM)
        o_c = gqa_softmax(qg, k_c, v_c, sink=sink).reshape(bsz, n_ctx, -1)
        y_c = jnp.concatenate([a_c.astype(o_c.dtype), o_c], axis=-1) @ w_out
    return y_c, y_l


def mixer_c(hc, hl, w_in, w_out, q_gain, k_gain, rope, ctx_out):
    bsz, n_ctx, _ = hc.shape
    n_lat = hl.shape[1]
    proj = jnp.concatenate([hc, hl], axis=1) @ w_in
    o1 = C_HEADS * HEAD_DIM
    o2 = o1 + C_KV_HEADS * HEAD_DIM

    def split(p):
        t = p.shape[1]
        q = rmsnorm(p[..., :o1].reshape(bsz, t, C_HEADS, HEAD_DIM), q_gain)
        k = rmsnorm(p[..., o1:o2].reshape(bsz, t, C_KV_HEADS, HEAD_DIM), k_gain)
        v = p[..., o2:].reshape(bsz, t, C_KV_HEADS, HEAD_DIM)
        return q, k, v

    q_c, k_c, v_c = split(proj[:, :n_ctx])
    q_l, k_l, v_l = split(proj[:, n_ctx:])
    q_l = apply_axial_rope(q_l, rope)
    k_l = apply_axial_rope(k_l, rope)
    k_all = jnp.concatenate([k_l, k_c], axis=1)
    v_all = jnp.concatenate([v_l, v_c], axis=1)
    nb = n_lat // BLOCK
    q_blk = jnp.moveaxis(q_l.reshape(bsz, nb, BLOCK, C_KV_HEADS, C_GROUP, HEAD_DIM), 1, 0)
    o = lax.map(lambda qn: gqa_softmax(qn, k_all, v_all), q_blk)
    y_l = jnp.moveaxis(o, 0, 1).reshape(bsz, n_lat, C_OUT) @ w_out
    y_c = None
    if ctx_out:
        qg = q_c.reshape(bsz, n_ctx, C_KV_HEADS, C_GROUP, HEAD_DIM)
        y_c = gqa_softmax(qg, k_c, v_c).reshape(bsz, n_ctx, C_OUT) @ w_out
    return y_c, y_l


def swiglu(h, w1, w3, w2):
    return (jax.nn.silu(h @ w1) * (h @ w3)) @ w2


def moe_swiglu(h, router_w, w1, w3, w2):
    logits = (h @ router_w).astype(jnp.float32)
    top_v, top_i = lax.top_k(logits, TOP_K)
    top_w = jax.nn.softmax(top_v, axis=-1)
    out = jnp.zeros_like(h)
    for e in range(N_EXPERTS):
        gate_e = jnp.sum(jnp.where(top_i == e, top_w, 0.0), axis=-1)
        out = out + (gate_e[..., None] * swiglu(h, w1[e], w3[e], w2[e])).astype(h.dtype)
    return out


def setup_inputs(seed: int = 0) -> dict:
    key = jax.random.key(seed)
    ks = jax.random.split(key, 40)
    D = D_MODEL
    G, P, H = S5_GROUPS, S5_STATE, S5_GROUP

    def nrm(i, shape, std):
        return jax.random.normal(ks[i], shape, jnp.float32) * std

    lam_im0 = jnp.pi * jnp.arange(P, dtype=jnp.float32)
    return {
        "x": nrm(0, (BATCH, SEQ, D), 1.0),
        "c": nrm(1, (BATCH, D), 1.0),
        "ctx": nrm(2, (BATCH, CTX_LEN, D), 1.0),
        "c_ctx": nrm(3, (D,), 1.0),
        "w_mod": nrm(4, (DEPTH, D, N_MOD * D), 0.5 * D ** -0.5),
        "b_mod": nrm(5, (DEPTH, N_MOD * D), 0.02),
        "norm_g": 1.0 + nrm(6, (DEPTH, 4, D), 0.05),
        "ab_w_in": nrm(7, (N_EVEN, D, AB_IN), D ** -0.5),
        "ab_w_out": nrm(8, (N_EVEN, AB_OUT, D), AB_OUT ** -0.5),
        "s5_lam_re": -0.5 + nrm(9, (N_EVEN, 2, G, P), 0.01),
        "s5_lam_im": lam_im0 + nrm(10, (N_EVEN, 2, G, P), 0.01),
        "s5_log_dt": jax.random.uniform(ks[11], (N_EVEN, 2, G), jnp.float32, math.log(1e-3), math.log(1e-1)),
        "s5_b_re": nrm(12, (N_EVEN, 2, G, P, H), (2 * H) ** -0.5),
        "s5_b_im": nrm(13, (N_EVEN, 2, G, P, H), (2 * H) ** -0.5),
        "s5_c_re": nrm(14, (N_EVEN, 2, G, H, P), (2 * P) ** -0.5),
        "s5_c_im": nrm(15, (N_EVEN, 2, G, H, P), (2 * P) ** -0.5),
        "s5_d": nrm(16, (N_EVEN, G, H), 1.0),
        "glu_w": nrm(17, (N_EVEN, S5_WIDTH, S5_WIDTH), S5_WIDTH ** -0.5),
        "glu_b": nrm(18, (N_EVEN, S5_WIDTH), 0.02),
        "attn_sink": nrm(19, (N_EVEN, B_HEADS), 1.0),
        "ffn_w1": nrm(20, (N_EVEN, D, FFN_DENSE), D ** -0.5),
        "ffn_w3": nrm(21, (N_EVEN, D, FFN_DENSE), D ** -0.5),
        "ffn_w2": nrm(22, (N_EVEN, FFN_DENSE, D), FFN_DENSE ** -0.5),
        "c_w_in": nrm(23, (N_ODD, D, C_IN), D ** -0.5),
        "c_w_out": nrm(24, (N_ODD, C_OUT, D), C_OUT ** -0.5),
        "q_norm": 1.0 + nrm(25, (N_ODD, HEAD_DIM), 0.05),
        "k_norm": 1.0 + nrm(26, (N_ODD, HEAD_DIM), 0.05),
        "router_w": nrm(27, (N_ODD, D, N_EXPERTS), D ** -0.5),
        "moe_w1": nrm(28, (N_ODD, N_EXPERTS, D, FFN_EXPERT), D ** -0.5),
        "moe_w3": nrm(29, (N_ODD, N_EXPERTS, D, FFN_EXPERT), D ** -0.5),
        "moe_w2": nrm(30, (N_ODD, N_EXPERTS, FFN_EXPERT, D), FFN_EXPERT ** -0.5),
    }


def reference(x, c, ctx, c_ctx, w_mod, b_mod, norm_g, ab_w_in, ab_w_out, s5_lam_re, s5_lam_im, s5_log_dt,
              s5_b_re, s5_b_im, s5_c_re, s5_c_im, s5_d, glu_w, glu_b, attn_sink, ffn_w1, ffn_w3, ffn_w2,
              c_w_in, c_w_out, q_norm, k_norm, router_w, moe_w1, moe_w3, moe_w2):
    n_lat = x.shape[1]
    rope = axial_rope_tables(n_lat)
    xl = x
    xc = ctx
    for i in range(DEPTH):
        last = i == DEPTH - 1
        ctx_out = not last
        j = i // 2
        mod_l = adaln(c, w_mod[i], b_mod[i])
        mod_c = adaln(c_ctx[None, :], w_mod[i], b_mod[i])
        hl = modulate(rmsnorm(xl, norm_g[i, 0]), mod_l[:, 0], mod_l[:, 1])
        hc = modulate(rmsnorm(xc, norm_g[i, 0]), mod_c[:, 0], mod_c[:, 1])
        if i % 2 == 0:
            yc, yl = mixer_ab(hc, hl, ab_w_in[j], ab_w_out[j], s5_lam_re[j], s5_lam_im[j], s5_log_dt[j],
                              s5_b_re[j], s5_b_im[j], s5_c_re[j], s5_c_im[j], s5_d[j], glu_w[j], glu_b[j],
                              attn_sink[j], rope, ctx_out)
        else:
            yc, yl = mixer_c(hc, hl, c_w_in[j], c_w_out[j], q_norm[j], k_norm[j], rope, ctx_out)
        xl = xl + mod_l[:, None, 2] * rmsnorm(yl, norm_g[i, 1])
        if ctx_out:
            xc = xc + mod_c[:, None, 2] * rmsnorm(yc, norm_g[i, 1])
        hl = modulate(rmsnorm(xl, norm_g[i, 2]), mod_l[:, 3], mod_l[:, 4])
        if ctx_out:
            hc = modulate(rmsnorm(xc, norm_g[i, 2]), mod_c[:, 3], mod_c[:, 4])
            h = jnp.concatenate([hc, hl], axis=1)
        else:
            h = hl
        if i % 2 == 0:
            f = swiglu(h, ffn_w1[j], ffn_w3[j], ffn_w2[j])
        else:
            f = moe_swiglu(h, router_w[j], moe_w1[j], moe_w3[j], moe_w2[j])
        n_c = h.shape[1] - n_lat
        xl = xl + mod_l[:, None, 5] * rmsnorm(f[:, n_c:], norm_g[i, 3])
        if ctx_out:
            xc = xc + mod_c[:, None, 5] * rmsnorm(f[:, :n_c], norm_g[i, 3])
    return xl
```

```python
import functools
import math

import jax
import jax.numpy as jnp
from jax import lax
from jax.experimental import pallas as pl
from jax.experimental.pallas import tpu as pltpu

F32 = jnp.float32
BF16 = jnp.bfloat16

HEAD_DIM = 128
GRID_W = 64
ROPE_THETA = 10000.0
NORM_EPS = 1e-6
NEG = -1e30
N_MOD = 6
S5_GROUP = 16
S5_STATE = 64
S5_CHUNK = 16
WINDOW = 128
TOP_K = 2
LANES = 128
ROW_TILE = 256
MIB = 1024 * 1024


def _cp(sem, vmem_mib):
    return pltpu.CompilerParams(dimension_semantics=sem, vmem_limit_bytes=vmem_mib * MIB)


def _pick(n, cands):
    for t in cands:
        if n % t == 0:
            return t
    raise ValueError(f"no tile for {n} in {cands}")


def _rms(v):
    return v * lax.rsqrt(jnp.mean(v * v, axis=-1, keepdims=True) + NORM_EPS)


def _adaln_body(cond_ref, w_ref, b_ref, o_ref):
    c = cond_ref[...]
    a = (c * jax.nn.sigmoid(c)).astype(BF16)
    o_ref[...] = jnp.dot(a, w_ref[...].astype(BF16), preferred_element_type=F32) + b_ref[...]


def _adaln(cond8, w_mod, b_mod):
    depth, d, n6 = w_mod.shape
    tn = 512
    return pl.pallas_call(
        _adaln_body,
        grid=(depth, n6 // tn),
        in_specs=[pl.BlockSpec((8, d), lambda l, j: (0, 0)),
                  pl.BlockSpec((None, d, tn), lambda l, j: (l, 0, j)),
                  pl.BlockSpec((None, 1, tn), lambda l, j: (l, 0, j))],
        out_specs=pl.BlockSpec((None, 8, tn), lambda l, j: (l, 0, j)),
        out_shape=jax.ShapeDtypeStruct((depth, 8, n6), F32),
        compiler_params=_cp(("parallel", "parallel"), 40),
        name="adaln",
    )(cond8, w_mod, b_mod.reshape(depth, 1, n6))


def _prenorm_body(x_ref, ctx_ref, g_ref, mod_ref, xa_ref, h_ref):
    i = pl.program_id(0)

    def emit(v):
        xa_ref[...] = v
        n = _rms(v) * g_ref[0:1, :]
        h_ref[...] = (n * (1.0 + mod_ref[1:2, :]) + mod_ref[0:1, :]).astype(BF16)

    @pl.when(i == 0)
    def _():
        emit(ctx_ref[...])

    @pl.when(i > 0)
    def _():
        emit(x_ref[...])


def _prenorm(x2, ctx2, norm_g, mod4):
    seq, d = x2.shape
    nc = ctx2.shape[0]
    nt = seq + nc
    tr = ROW_TILE
    return pl.pallas_call(
        _prenorm_body,
        grid=(nt // tr,),
        in_specs=[pl.BlockSpec((tr, d), lambda i: (jnp.maximum(i - 1, 0), 0)),
                  pl.BlockSpec((tr, d), lambda i: (0, 0)),
                  pl.BlockSpec((None, 4, d), lambda i: (0, 0, 0)),
                  pl.BlockSpec((None, None, N_MOD, d), lambda i: (0, jnp.minimum(i, 1), 0, 0))],
        out_specs=[pl.BlockSpec((tr, d), lambda i: (i, 0)),
                   pl.BlockSpec((tr, d), lambda i: (i, 0))],
        out_shape=[jax.ShapeDtypeStruct((nt, d), F32), jax.ShapeDtypeStruct((nt, d), BF16)],
        compiler_params=_cp(("parallel",), 40),
        name="prenorm",
    )(x2, ctx2, norm_g, mod4)


def _resid_body(xa_ref, y_ref, gcur_ref, mcur_ref, gnext_ref, mnext_ref, xo_ref, h_ref, *,
                post_idx, gate_idx, next_g_idx, next_shift_idx):
    yn = _rms(y_ref[...].astype(F32)) * gcur_ref[post_idx:post_idx + 1, :]
    x = xa_ref[...] + mcur_ref[gate_idx:gate_idx + 1, :] * yn
    xo_ref[...] = x
    n = _rms(x) * gnext_ref[next_g_idx:next_g_idx + 1, :]
    h_ref[...] = (n * (1.0 + mnext_ref[next_shift_idx + 1:next_shift_idx + 2, :])
                  + mnext_ref[next_shift_idx:next_shift_idx + 1, :]).astype(BF16)


def _resid(xa, y, norm_g, mod4, layer, post_idx, gate_idx, next_layer, next_g_idx, next_shift_idx):
    nt, d = xa.shape
    tr = ROW_TILE
    row = lambda i: (i, 0)
    body = functools.partial(_resid_body, post_idx=post_idx, gate_idx=gate_idx,
                             next_g_idx=next_g_idx, next_shift_idx=next_shift_idx)
    return pl.pallas_call(
        body,
        grid=(nt // tr,),
        in_specs=[pl.BlockSpec((tr, d), row),
                  pl.BlockSpec((tr, d), row),
                  pl.BlockSpec((None, 4, d), lambda i: (layer, 0, 0)),
                  pl.BlockSpec((None, None, N_MOD, d), lambda i: (layer, jnp.minimum(i, 1), 0, 0)),
                  pl.BlockSpec((None, 4, d), lambda i: (next_layer, 0, 0)),
                  pl.BlockSpec((None, None, N_MOD, d), lambda i: (next_layer, jnp.minimum(i, 1), 0, 0))],
        out_specs=[pl.BlockSpec((tr, d), row), pl.BlockSpec((tr, d), row)],
        out_shape=[jax.ShapeDtypeStruct((nt, d), F32), jax.ShapeDtypeStruct((nt, d), BF16)],
        input_output_aliases={0: 0},
        compiler_params=_cp(("parallel",), 48),
        name="resid",
    )(xa, y, norm_g, mod4, norm_g, mod4)


def _final_body(xa_ref, y_ref, g_ref, m_ref, o_ref, *, post_idx, gate_idx):
    yn = _rms(y_ref[...].astype(F32)) * g_ref[post_idx:post_idx + 1, :]
    o_ref[...] = xa_ref[...] + m_ref[gate_idx:gate_idx + 1, :] * yn


def _final(xa, y, norm_g, mod4, layer, post_idx, gate_idx, n_ctx):
    nt, d = xa.shape
    tr = ROW_TILE
    off = n_ctx // tr
    body = functools.partial(_final_body, post_idx=post_idx, gate_idx=gate_idx)
    return pl.pallas_call(
        body,
        grid=((nt - n_ctx) // tr,),
        in_specs=[pl.BlockSpec((tr, d), lambda i: (i + off, 0)),
                  pl.BlockSpec((tr, d), lambda i: (i + off, 0)),
                  pl.BlockSpec((None, 4, d), lambda i: (layer, 0, 0)),
                  pl.BlockSpec((None, None, N_MOD, d), lambda i: (layer, 1, 0, 0))],
        out_specs=pl.BlockSpec((tr, d), lambda i: (i, 0)),
        out_shape=jax.ShapeDtypeStruct((nt - n_ctx, d), F32),
        compiler_params=_cp(("parallel",), 40),
        name="final",
    )(xa, y, norm_g, mod4)


def _inproj_body(h_ref, w_ref, cos_ref, sin_ref, gain_ref, scale_ref, o_ref, *, rope_lo, rope_hi, use_norm):
    j = pl.program_id(1)
    acc = jnp.dot(h_ref[...], w_ref[...], preferred_element_type=F32)
    is_rope = jnp.logical_and(j >= rope_lo, j < rope_hi)

    @pl.when(jnp.logical_not(is_rope))
    def _():
        o_ref[...] = acc.astype(BF16)

    @pl.when(is_rope)
    def _():
        c = cos_ref[...]
        s = sin_ref[...]
        lane = lax.broadcasted_iota(jnp.int32, c.shape, 1)
        first = (lane & 32) == 0
        for hh in range(acc.shape[1] // HEAD_DIM):
            sl = slice(hh * HEAD_DIM, (hh + 1) * HEAD_DIM)
            blk = acc[:, sl]
            if use_norm:
                blk = _rms(blk) * gain_ref[:, sl]
            rot = jnp.where(first, pltpu.roll(blk, HEAD_DIM - 32, 1), pltpu.roll(blk, 32, 1))
            o_ref[:, sl] = ((blk * c + rot * s) * scale_ref[:, sl]).astype(BF16)


def _inproj(h, w, cos_t, sin_t, gain, scale, rope_lo, rope_hi, use_norm):
    nt, d = h.shape
    n = w.shape[1]
    tm = _pick(nt, (1280, 640, 256))
    tn = 512
    body = functools.partial(_inproj_body, rope_lo=rope_lo // tn, rope_hi=rope_hi // tn, use_norm=use_norm)
    return pl.pallas_call(
        body,
        grid=(nt // tm, n // tn),
        in_specs=[pl.BlockSpec((tm, d), lambda i, j: (i, 0)),
                  pl.BlockSpec((d, tn), lambda i, j: (0, j)),
                  pl.BlockSpec((tm, HEAD_DIM), lambda i, j: (i, 0)),
                  pl.BlockSpec((tm, HEAD_DIM), lambda i, j: (i, 0)),
                  pl.BlockSpec((1, tn), lambda i, j: (0, j)),
                  pl.BlockSpec((1, tn), lambda i, j: (0, j))],
        out_specs=pl.BlockSpec((tm, tn), lambda i, j: (i, j)),
        out_shape=jax.ShapeDtypeStruct((nt, n), BF16),
        compiler_params=_cp(("parallel", "arbitrary"), 48),
        name="inproj",
    )(h, w, cos_t, sin_t, gain, scale)


def _mm_body(*refs, n_a):
    o_ref = refs[2 * n_a]
    acc = jnp.dot(refs[0][...], refs[n_a][...], preferred_element_type=F32)
    for t in range(1, n_a):
        acc += jnp.dot(refs[t][...], refs[n_a + t][...], preferred_element_type=F32)
    o_ref[...] = acc.astype(o_ref.dtype)


def _matmul(a_list, w_list, tm_cands, tn, out_dtype, vmem_mib, name):
    nt = a_list[0].shape[0]
    n = w_list[0].shape[1]
    tm = _pick(nt, tm_cands)
    in_specs = [pl.BlockSpec((tm, a.shape[1]), lambda i, j: (i, 0)) for a in a_list]
    in_specs += [pl.BlockSpec((w.shape[0], tn), lambda i, j: (0, j)) for w in w_list]
    return pl.pallas_call(
        functools.partial(_mm_body, n_a=len(a_list)),
        grid=(nt // tm, n // tn),
        in_specs=in_specs,
        out_specs=pl.BlockSpec((tm, tn), lambda i, j: (i, j)),
        out_shape=jax.ShapeDtypeStruct((nt, n), out_dtype),
        compiler_params=_cp(("parallel", "arbitrary"), vmem_mib),
        name=name,
    )(*a_list, *w_list)


def _swiglu_up_body(*refs, gated):
    if gated:
        h_ref, w1_ref, w3_ref, gate_ref, o_ref = refs
    else:
        h_ref, w1_ref, w3_ref, o_ref = refs
    h = h_ref[...]
    a = jnp.dot(h, w1_ref[...], preferred_element_type=F32)
    b = jnp.dot(h, w3_ref[...], preferred_element_type=F32)
    r = a * jax.nn.sigmoid(a) * b
    if gated:
        g = gate_ref[...]
        lane = lax.broadcasted_iota(jnp.int32, g.shape, 1)
        r = r * jnp.sum(jnp.where(lane == pl.program_id(1), g, 0.0), axis=-1, keepdims=True)
    o_ref[...] = r.astype(BF16)


def _swiglu_up(h, w1, w3, gates=None):
    nt, d = h.shape
    n_e, _, f = w1.shape
    gated = gates is not None
    if gated:
        tm, tn = _pick(nt, (640, 256)), f
        w_spec = pl.BlockSpec((None, d, tn), lambda i, j: (j, 0, 0))
    else:
        tm, tn = _pick(nt, (1280, 640, 256)), 512
        w_spec = pl.BlockSpec((None, d, tn), lambda i, j: (0, 0, j))
    in_specs = [pl.BlockSpec((tm, d), lambda i, j: (i, 0)), w_spec, w_spec]
    args = [h, w1, w3]
    if gated:
        in_specs.append(pl.BlockSpec((tm, LANES), lambda i, j: (i, 0)))
        args.append(gates)
    return pl.pallas_call(
        functools.partial(_swiglu_up_body, gated=gated),
        grid=(nt // tm, n_e * f // tn),
        in_specs=in_specs,
        out_specs=pl.BlockSpec((tm, tn), lambda i, j: (i, j)),
        out_shape=jax.ShapeDtypeStruct((nt, n_e * f), BF16),
        compiler_params=_cp(("parallel", "arbitrary"), 52),
        name="swiglu_up",
    )(*args)


def _router_body(h_ref, whi_ref, wlo_ref, g_ref, *, n_experts):
    h = h_ref[...]
    logits = (jnp.dot(h, whi_ref[...], preferred_element_type=F32)
              + jnp.dot(h, wlo_ref[...], preferred_element_type=F32))
    lane = lax.broadcasted_iota(jnp.int32, logits.shape, 1)
    lg = jnp.where(lane < n_experts, logits, NEG)
    v1 = jnp.max(lg, axis=-1, keepdims=True)
    i1 = jnp.min(jnp.where(lg == v1, lane, LANES), axis=-1, keepdims=True)
    lg2 = jnp.where(lane == i1, NEG, lg)
    v2 = jnp.max(lg2, axis=-1, keepdims=True)
    i2 = jnp.min(jnp.where(lg2 == v2, lane, LANES), axis=-1, keepdims=True)
    e = jnp.exp(v2 - v1)
    den = 1.0 + e
    g_ref[...] = jnp.where(lane == i1, 1.0 / den, 0.0) + jnp.where(lane == i2, e / den, 0.0)


def _router(h, router_w):
    nt, d = h.shape
    n_e = router_w.shape[1]
    whi = router_w.astype(BF16)
    wlo = (router_w - whi.astype(F32)).astype(BF16)
    pad = ((0, 0), (0, LANES - n_e))
    whi = jnp.pad(whi, pad)
    wlo = jnp.pad(wlo, pad)
    tm = _pick(nt, (640, 256))
    return pl.pallas_call(
        functools.partial(_router_body, n_experts=n_e),
        grid=(nt // tm,),
        in_specs=[pl.BlockSpec((tm, d), lambda i: (i, 0)),
                  pl.BlockSpec((d, LANES), lambda i: (0, 0)),
                  pl.BlockSpec((d, LANES), lambda i: (0, 0))],
        out_specs=pl.BlockSpec((tm, LANES), lambda i: (i, 0)),
        out_shape=jax.ShapeDtypeStruct((nt, LANES), F32),
        compiler_params=_cp(("parallel",), 32),
        name="router",
    )(h, whi, wlo)


def _stack_heads(q, n_grp):
    return jnp.concatenate([q[:, g * HEAD_DIM:(g + 1) * HEAD_DIM] for g in range(n_grp)], axis=0)


def _unstack_store(o_ref, o, n_grp, rows):
    for g in range(n_grp):
        o_ref[:, g * HEAD_DIM:(g + 1) * HEAD_DIM] = o[g * rows:(g + 1) * rows, :].astype(o_ref.dtype)


def _qk(q, k):
    return lax.dot_general(q, k, (((1,), (1,)), ((), ())), preferred_element_type=F32)


def _win_body(sink_ref, q_ref, km_ref, k0_ref, kp_ref, kc_ref, vm_ref, v0_ref, vp_ref, vc_ref, o_ref, *,
              n_grp, ctx_blocks, n_lat):
    kvh = pl.program_id(0)
    n = pl.program_id(1)
    blk = q_ref.shape[0]
    qs = _stack_heads(q_ref[...], n_grp)
    kw = jnp.concatenate([km_ref[...], k0_ref[...], kp_ref[...]], axis=0)
    vw = jnp.concatenate([vm_ref[...], v0_ref[...], vp_ref[...]], axis=0)
    s_w = _qk(qs, kw)
    s_c = _qk(qs, kc_ref[...])
    q_off = lax.broadcasted_iota(jnp.int32, s_w.shape, 0) % blk
    k_off = lax.broadcasted_iota(jnp.int32, s_w.shape, 1) - blk
    kpos = (n - ctx_blocks) * blk + k_off
    valid = (jnp.abs(q_off - k_off) <= WINDOW) & (kpos >= 0) & (kpos < n_lat) & (n >= ctx_blocks)
    s_w = jnp.where(valid, s_w, NEG)
    sink = jnp.concatenate(
        [jnp.full((blk, 1), sink_ref[kvh * n_grp + g], F32) for g in range(n_grp)], axis=0)
    m = jnp.maximum(jnp.maximum(jnp.max(s_w, axis=-1, keepdims=True),
                                jnp.max(s_c, axis=-1, keepdims=True)), sink)
    p_w = jnp.exp(s_w - m)
    p_c = jnp.exp(s_c - m)
    den = (jnp.sum(p_w, axis=-1, keepdims=True) + jnp.sum(p_c, axis=-1, keepdims=True)
           + jnp.exp(sink - m))
    o = (jnp.dot(p_w.astype(BF16), vw, preferred_element_type=F32)
         + jnp.dot(p_c.astype(BF16), vc_ref[...], preferred_element_type=F32)) / den
    _unstack_store(o_ref, o, n_grp, blk)


def _window_attention(proj, sink, n_ctx, q_col, k_col, v_col, n_kv, n_grp):
    nt = proj.shape[0]
    blk = WINDOW
    nb = nt // blk
    cb = n_ctx // blk
    qc, kc, vc = q_col // (n_grp * HEAD_DIM), k_col // HEAD_DIM, v_col // HEAD_DIM
    lo = lambda n: jnp.maximum(n - 1, cb)
    hi = lambda n: jnp.minimum(n + 1, nb - 1)
    tile = lambda rowf, col0: pl.BlockSpec((blk, HEAD_DIM), lambda h, n: (rowf(n), col0 + h))
    ctx_tile = lambda col0: pl.BlockSpec((n_ctx, HEAD_DIM), lambda h, n: (0, col0 + h))
    same = lambda n: n
    body = functools.partial(_win_body, n_grp=n_grp, ctx_blocks=cb, n_lat=nt - n_ctx)
    return pl.pallas_call(
        body,
        grid=(n_kv, nb),
        in_specs=[pl.BlockSpec(memory_space=pltpu.SMEM),
                  pl.BlockSpec((blk, n_grp * HEAD_DIM), lambda h, n: (n, qc + h)),
                  tile(lo, kc), tile(same, kc), tile(hi, kc), ctx_tile(kc),
                  tile(lo, vc), tile(same, vc), tile(hi, vc), ctx_tile(vc)],
        out_specs=pl.BlockSpec((blk, n_grp * HEAD_DIM), lambda h, n: (n, h)),
        out_shape=jax.ShapeDtypeStruct((nt, n_kv * n_grp * HEAD_DIM), BF16),
        compiler_params=_cp(("parallel", "parallel"), 32),
        name="window_attn",
    )(sink, proj, proj, proj, proj, proj, proj, proj, proj, proj)


def _flash_body(q_ref, k_ref, v_ref, o_ref, m_ref, l_ref, acc_ref, *, n_grp, tk, n_chunks, n_ctx):
    qi = pl.program_id(1)
    tq = q_ref.shape[0]
    qs = _stack_heads(q_ref[...], n_grp)
    is_ctx = qi == 0

    def chunk(start, limit):
        k = k_ref[pl.ds(start, tk), :]
        v = v_ref[pl.ds(start, tk), :]
        s = _qk(qs, k)
        if limit is not None:
            col = lax.broadcasted_iota(jnp.int32, s.shape, 1)
            s = jnp.where(col < limit, s, NEG)
        m_old = m_ref[...]
        m_new = jnp.maximum(m_old, jnp.max(s, axis=-1, keepdims=True))
        alpha = jnp.exp(m_old - m_new)
        p = jnp.exp(s - m_new)
        l_ref[...] = alpha * l_ref[...] + jnp.sum(p, axis=-1, keepdims=True)
        acc_ref[...] = alpha * acc_ref[...] + jnp.dot(p.astype(BF16), v, preferred_element_type=F32)
        m_ref[...] = m_new

    m_ref[...] = jnp.full(m_ref.shape, NEG, F32)
    l_ref[...] = jnp.zeros(l_ref.shape, F32)
    acc_ref[...] = jnp.zeros(acc_ref.shape, F32)
    chunk(0, jnp.where(is_ctx, n_ctx, tk))

    def loop(c, carry):
        chunk(pl.multiple_of(c * tk, tk), None)
        return carry

    lax.fori_loop(1, jnp.where(is_ctx, 1, n_chunks), loop, 0)
    _unstack_store(o_ref, acc_ref[...] / l_ref[...], n_grp, tq)


def _flash_attention(proj, n_ctx, q_col, k_col, v_col, n_kv, n_grp):
    nt = proj.shape[0]
    tq = n_ctx
    tk = _pick(nt, (640, 256))
    qc, kc, vc = q_col // (n_grp * HEAD_DIM), k_col // HEAD_DIM, v_col // HEAD_DIM
    m_rows = n_grp * tq
    body = functools.partial(_flash_body, n_grp=n_grp, tk=tk, n_chunks=nt // tk, n_ctx=n_ctx)
    return pl.pallas_call(
        body,
        grid=(n_kv, nt // tq),
        in_specs=[pl.BlockSpec((tq, n_grp * HEAD_DIM), lambda h, i: (i, qc + h)),
                  pl.BlockSpec((nt, HEAD_DIM), lambda h, i: (0, kc + h)),
                  pl.BlockSpec((nt, HEAD_DIM), lambda h, i: (0, vc + h))],
        out_specs=pl.BlockSpec((tq, n_grp * HEAD_DIM), lambda h, i: (i, h)),
        out_shape=jax.ShapeDtypeStruct((nt, n_kv * n_grp * HEAD_DIM), BF16),
        scratch_shapes=[pltpu.VMEM((m_rows, 1), F32), pltpu.VMEM((m_rows, 1), F32),
                        pltpu.VMEM((m_rows, HEAD_DIM), F32)],
        compiler_params=_cp(("parallel", "parallel"), 48),
        name="flash_attn",
    )(proj, proj, proj)


def _s5_weights(lam_re, lam_im, log_dt, b_re, b_im, c_re, c_im, d_skip):
    hp = lax.Precision.HIGHEST
    n_g, n_p = lam_re.shape[1], lam_re.shape[2]
    n_h = b_re.shape[-1]
    ell = S5_CHUNK
    dt = jnp.exp(log_dt)[..., None]
    ar, ai = lam_re * dt, lam_im * dt
    k = jnp.arange(ell + 1, dtype=F32)[:, None, None, None]
    mag = jnp.exp(k * ar[None])
    lp_re, lp_im = mag * jnp.cos(k * ai[None]), mag * jnp.sin(k * ai[None])
    nr, ni = lp_re[1] - 1.0, lp_im[1]
    den = lam_re * lam_re + lam_im * lam_im
    fr, fi = (nr * lam_re + ni * lam_im) / den, (ni * lam_re - nr * lam_im) / den
    bb_re = fr[..., None] * b_re - fi[..., None] * b_im
    bb_im = fr[..., None] * b_im + fi[..., None] * b_re
    lpe_re, lpe_im = lp_re[:, :, :, None, :], lp_im[:, :, :, None, :]
    cl_re = c_re[None] * lpe_re - c_im[None] * lpe_im
    cl_im = c_re[None] * lpe_im + c_im[None] * lpe_re
    m = (jnp.einsum('kdghp,dgpi->kdghi', cl_re, bb_re, precision=hp)
         - jnp.einsum('kdghp,dgpi->kdghi', cl_im, bb_im, precision=hp))
    t_in = jnp.arange(ell)[:, None]
    t_out = jnp.arange(ell)[None, :]
    lag = t_out - t_in
    sel = lambda cond: cond[:, :, None, None, None]
    toep = (jnp.where(sel(lag >= 0), m[:, 0][jnp.clip(lag, 0, ell)], 0.0)
            + jnp.where(sel(lag <= 0), m[:, 1][jnp.clip(-lag, 0, ell)], 0.0))
    toep = toep.transpose(2, 0, 4, 1, 3).reshape(n_g // 2, 2, ell * n_h, ell * n_h)
    eye2 = jnp.eye(2, dtype=F32)
    toep2 = jnp.einsum('pgab,gh->pgahb', toep, eye2).reshape(n_g // 2, 2 * ell * n_h, 2 * ell * n_h)

    def state_in(direction, powers):
        pr = lp_re[powers, direction].transpose(1, 0, 2)[:, :, None, :]
        pi = lp_im[powers, direction].transpose(1, 0, 2)[:, :, None, :]
        br = bb_re[direction].transpose(0, 2, 1)[:, None, :, :]
        bi = bb_im[direction].transpose(0, 2, 1)[:, None, :, :]
        return pr * br - pi * bi, pr * bi + pi * br

    steps = jnp.arange(ell)
    f_re, f_im = state_in(0, ell - 1 - steps)
    g_re, g_im = state_in(1, steps)
    w_in = jnp.stack([f_re, f_im, g_re, g_im], axis=3)
    w_in = w_in.reshape(n_g // 2, 2, ell * n_h, 4, n_p)
    w_in2 = jnp.einsum('pgakn,gh->pgakhn', w_in, eye2).reshape(n_g // 2, 2 * ell * n_h, 4 * 2 * n_p)

    def state_out(direction, powers):
        wr = cl_re[powers, direction].transpose(1, 3, 0, 2)
        wi = -cl_im[powers, direction].transpose(1, 3, 0, 2)
        return jnp.stack([wr, wi], axis=1)

    def pair_rows(w):
        w = w.reshape(n_g // 2, 2, 2, n_p, ell * n_h)
        return jnp.einsum('pgkna,gh->pkgnha', w, eye2).reshape(n_g // 2, 2 * 2 * n_p, 2 * ell * n_h)

    w_of = pair_rows(state_out(0, steps + 1))
    w_ob = pair_rows(state_out(1, ell - steps))
    lam_l = jnp.stack([lp_re[ell, 0], lp_im[ell, 0], lp_re[ell, 1], lp_im[ell, 1]], axis=0)
    lam_l = lam_l.reshape(4, n_g // 2, 2 * n_p)
    d2 = jnp.broadcast_to(d_skip.reshape(n_g // 2, 2, 1, n_h), (n_g // 2, 2, ell, n_h))
    d2 = d2.reshape(n_g // 2, 1, 2 * ell * n_h)
    return toep2.astype(BF16), w_in2.astype(BF16), w_of.astype(BF16), w_ob.astype(BF16), lam_l, d2


def _s5_in_body(u_ref, w_ref, x_ref):
    x_ref[...] = jnp.dot(u_ref[...], w_ref[...], preferred_element_type=F32)


def _s5_scan_body(xf_ref, xb_ref, lam_ref, sf_ref, sb_ref, st_ref):
    @pl.when(pl.program_id(0) == 0)
    def _():
        st_ref[...] = jnp.zeros(st_ref.shape, F32)

    half = xf_ref.shape[2] // 2
    rows = xf_ref.shape[0]

    def run(x_ref, s_ref, base, order):
        ar, ai = lam_ref[base], lam_ref[base + 1]
        sr, si = st_ref[base], st_ref[base + 1]
        for r in order:
            s_ref[r, :, 0:half] = sr
            s_ref[r, :, half:2 * half] = si
            xr, xi = x_ref[r, :, 0:half], x_ref[r, :, half:2 * half]
            sr, si = ar * sr - ai * si + xr, ar * si + ai * sr + xi
        st_ref[base] = sr
        st_ref[base + 1] = si

    run(xf_ref, sf_ref, 0, range(rows))
    run(xb_ref, sb_ref, 2, range(rows - 1, -1, -1))


def _s5_out_body(u_ref, sf_ref, sb_ref, toep_ref, wof_ref, wob_ref, d_ref, y_ref):
    u = u_ref[...]
    y = jnp.dot(u, toep_ref[...], preferred_element_type=F32)
    y += jnp.dot(sf_ref[...].astype(BF16), wof_ref[...], preferred_element_type=F32)
    y += jnp.dot(sb_ref[...].astype(BF16), wob_ref[...], preferred_element_type=F32)
    y_ref[...] = y + u.astype(F32) * d_ref[...]


def _s5(u, n_ctx, weights):
    toep2, w_in2, w_of, w_ob, lam_l, d2 = weights
    nt, width = u.shape
    ell = S5_CHUNK
    n_pair = toep2.shape[0]
    kdim = toep2.shape[1]
    sdim = w_in2.shape[2]
    n_chunk = nt // ell
    h = S5_GROUP
    u2 = u.reshape(n_chunk, ell, n_pair, 2, h).transpose(2, 0, 3, 1, 4).reshape(n_pair, n_chunk, kdim)
    per_pair = lambda r, c: pl.BlockSpec((None, r, c), lambda p: (p, 0, 0))
    x = pl.pallas_call(
        _s5_in_body,
        grid=(n_pair,),
        in_specs=[per_pair(n_chunk, kdim), per_pair(kdim, sdim)],
        out_specs=per_pair(n_chunk, sdim),
        out_shape=jax.ShapeDtypeStruct((n_pair, n_chunk, sdim), F32),
        compiler_params=_cp(("parallel",), 32),
        name="s5_state_in",
    )(u2, w_in2)
    xt = x.transpose(1, 0, 2)
    rb = n_ctx // ell
    nb = n_chunk // rb
    half = sdim // 2
    bwd = lambda b: (jnp.where(b == 0, 0, nb - b), 0, 0)
    sf, sb = pl.pallas_call(
        _s5_scan_body,
        grid=(nb,),
        in_specs=[pl.BlockSpec((rb, n_pair, half), lambda b: (b, 0, 0)),
                  pl.BlockSpec((rb, n_pair, half), lambda b: (jnp.where(b == 0, 0, nb - b), 0, 1)),
                  pl.BlockSpec((4, n_pair, half // 2), lambda b: (0, 0, 0))],
        out_specs=[pl.BlockSpec((rb, n_pair, half), lambda b: (b, 0, 0)),
                   pl.BlockSpec((rb, n_pair, half), bwd)],
        out_shape=[jax.ShapeDtypeStruct((n_chunk, n_pair, half), F32),
                   jax.ShapeDtypeStruct((n_chunk, n_pair, half), F32)],
        scratch_shapes=[pltpu.VMEM((4, n_pair, half // 2), F32)],
        compiler_params=_cp(("arbitrary",), 32),
        name="s5_scan",
    )(xt, xt, lam_l)
    y2 = pl.pallas_call(
        _s5_out_body,
        grid=(n_pair,),
        in_specs=[per_pair(n_chunk, kdim), per_pair(n_chunk, half), per_pair(n_chunk, half),
                  per_pair(kdim, kdim), per_pair(half, kdim), per_pair(half, kdim), per_pair(1, kdim)],
        out_specs=per_pair(n_chunk, kdim),
        out_shape=jax.ShapeDtypeStruct((n_pair, n_chunk, kdim), F32),
        compiler_params=_cp(("parallel",), 32),
        name="s5_out",
    )(u2, sf.transpose(1, 0, 2), sb.transpose(1, 0, 2), toep2, w_of, w_ob, d2)
    return y2.reshape(n_pair, n_chunk, 2, ell, h).transpose(1, 3, 0, 2, 4).reshape(nt, width)


def _glu_body(y_ref, w_ref, b_ref, o_ref):
    y = y_ref[...]
    gy = 0.5 * y * (1.0 + jnp.tanh(math.sqrt(2.0 / math.pi) * (y + 0.044715 * (y * y * y))))
    z = jnp.dot(gy.astype(BF16), w_ref[...], preferred_element_type=F32) + b_ref[...]
    o_ref[...] = (gy * jax.nn.sigmoid(z)).astype(BF16)


def _s5_glu(y, glu_w, glu_b):
    nt, width = y.shape
    tm = _pick(nt, (1280, 640, 256))
    return pl.pallas_call(
        _glu_body,
        grid=(nt // tm,),
        in_specs=[pl.BlockSpec((tm, width), lambda i: (i, 0)),
                  pl.BlockSpec((width, width), lambda i: (0, 0)),
                  pl.BlockSpec((1, width), lambda i: (0, 0))],
        out_specs=pl.BlockSpec((tm, width), lambda i: (i, 0)),
        out_shape=jax.ShapeDtypeStruct((nt, width), BF16),
        compiler_params=_cp(("parallel",), 40),
        name="s5_glu",
    )(y, glu_w, glu_b.reshape(1, width))


def _rope_tables(seq, n_ctx):
    t = jnp.arange(seq)
    row_id = (t // GRID_W).astype(F32)
    col_id = (t % GRID_W).astype(F32)
    n_freq = HEAD_DIM // 4
    inv = ROPE_THETA ** (-jnp.arange(n_freq, dtype=F32) / n_freq)
    ang_r = row_id[:, None] * inv[None, :]
    ang_c = col_id[:, None] * inv[None, :]
    cos_t = jnp.concatenate([jnp.cos(ang_r), jnp.cos(ang_r), jnp.cos(ang_c), jnp.cos(ang_c)], axis=1)
    sin_t = jnp.concatenate([-jnp.sin(ang_r), jnp.sin(ang_r), -jnp.sin(ang_c), jnp.sin(ang_c)], axis=1)
    cos_t = jnp.concatenate([jnp.ones((n_ctx, HEAD_DIM), F32), cos_t], axis=0)
    sin_t = jnp.concatenate([jnp.zeros((n_ctx, HEAD_DIM), F32), sin_t], axis=0)
    return cos_t, sin_t


def kernel(x, c, ctx, c_ctx, w_mod, b_mod, norm_g, ab_w_in, ab_w_out, s5_lam_re, s5_lam_im, s5_log_dt, s5_b_re, s5_b_im, s5_c_re, s5_c_im, s5_d, glu_w, glu_b, attn_sink, ffn_w1, ffn_w3, ffn_w2, c_w_in, c_w_out, q_norm, k_norm, router_w, moe_w1, moe_w3, moe_w2):
    assert x.shape[0] == 1 and ctx.shape[0] == 1
    seq, d = x.shape[1], x.shape[2]
    n_ctx = ctx.shape[1]
    assert n_ctx == ROW_TILE
    depth = w_mod.shape[0]
    s5_width = s5_d.shape[1] * s5_d.shape[2]
    b_heads = attn_sink.shape[1]
    b_kv = (ab_w_in.shape[2] - s5_width - b_heads * HEAD_DIM) // (2 * HEAD_DIM)
    c_heads = c_w_out.shape[1] // HEAD_DIM
    c_kv = (c_w_in.shape[2] - c_heads * HEAD_DIM) // (2 * HEAD_DIM)
    q_scale = HEAD_DIM ** -0.5

    cond8 = jnp.zeros((8, d), F32).at[0].set(c_ctx).at[1].set(c[0])
    mod4 = _adaln(cond8, w_mod, b_mod).reshape(depth, 8, N_MOD, d)
    cos_t, sin_t = _rope_tables(seq, n_ctx)
    xa, h = _prenorm(x[0], ctx[0], norm_g, mod4)

    def col_vec(pieces):
        return jnp.concatenate([jnp.broadcast_to(jnp.asarray(v, F32), (n,)) for v, n in pieces]).reshape(1, -1)

    for i in range(depth):
        j = i // 2
        if i % 2 == 0:
            q0 = s5_width
            k0 = q0 + b_heads * HEAD_DIM
            v0 = k0 + b_kv * HEAD_DIM
            n_in = v0 + b_kv * HEAD_DIM
            proj = _inproj(h, ab_w_in[j].astype(BF16), cos_t, sin_t,
                           jnp.ones((1, n_in), F32),
                           col_vec([(1.0, q0), (q_scale, k0 - q0), (1.0, n_in - k0)]),
                           q0, v0, False)
            s5w = _s5_weights(s5_lam_re[j], s5_lam_im[j], s5_log_dt[j], s5_b_re[j], s5_b_im[j],
                              s5_c_re[j], s5_c_im[j], s5_d[j])
            a = _s5_glu(_s5(proj[:, :s5_width], n_ctx, s5w), glu_w[j].astype(BF16), glu_b[j])
            o = _window_attention(proj, attn_sink[j], n_ctx, q0, k0, v0, b_kv, b_heads // b_kv)
            w_out = ab_w_out[j].astype(BF16)
            y = _matmul([a, o], [w_out[:s5_width], w_out[s5_width:]], (1280, 640, 256), 512, F32, 48, "ab_out")
        else:
            k0 = c_heads * HEAD_DIM
            v0 = k0 + c_kv * HEAD_DIM
            n_in = v0 + c_kv * HEAD_DIM
            proj = _inproj(h, c_w_in[j].astype(BF16), cos_t, sin_t,
                           col_vec([(jnp.tile(q_norm[j], c_heads), k0), (jnp.tile(k_norm[j], c_kv), v0 - k0),
                                    (1.0, n_in - v0)]),
                           col_vec([(q_scale, k0), (1.0, n_in - k0)]),
                           0, v0, True)
            o = _flash_attention(proj, n_ctx, 0, k0, v0, c_kv, c_heads // c_kv)
            y = _matmul([o], [c_w_out[j].astype(BF16)], (1280, 640, 256), 512, F32, 48, "c_out")
        xa, h = _resid(xa, y, norm_g, mod4, i, 1, 2, i, 2, 3)
        if i % 2 == 0:
            g = _swiglu_up(h, ffn_w1[j].astype(BF16)[None], ffn_w3[j].astype(BF16)[None])
            f = _matmul([g], [ffn_w2[j].astype(BF16)], (1280, 640, 256), 512, F32, 56, "ffn_down")
        else:
            gates = _router(h, router_w[j])
            g = _swiglu_up(h, moe_w1[j].astype(BF16), moe_w3[j].astype(BF16), gates)
            w2 = moe_w2[j].astype(BF16).reshape(-1, d)
            f = _matmul([g], [w2], (640, 256), 512, F32, 56, "moe_down")
        if i + 1 < depth:
            xa, h = _resid(xa, f, norm_g, mod4, i, 3, 5, i + 1, 0, 0)
        else:
            out = _final(xa, f, norm_g, mod4, i, 3, 5, n_ctx)
    return out[None]
```

```python
import functools
import math

import jax
import jax.numpy as jnp
from jax import lax
from jax.experimental import pallas as pl
from jax.experimental.pallas import tpu as pltpu

F32 = jnp.float32
BF16 = jnp.bfloat16

HEAD_DIM = 128
GRID_W = 64
ROPE_THETA = 10000.0
NORM_EPS = 1e-6
NEG = -1e30
N_MOD = 6
S5_GROUP = 16
S5_STATE = 64
S5_CHUNK = 16
WINDOW = 128
TOP_K = 2
LANES = 128
ROW_TILE = 256
MIB = 1024 * 1024


def _cp(sem, vmem_mib):
    return pltpu.CompilerParams(dimension_semantics=sem, vmem_limit_bytes=vmem_mib * MIB)


def _pick(n, cands):
    for t in cands:
        if n % t == 0:
            return t
    raise ValueError(f"no tile for {n} in {cands}")


def _rms(v):
    return v * lax.rsqrt(jnp.mean(v * v, axis=-1, keepdims=True) + NORM_EPS)


def _adaln_body(cond_ref, w_ref, b_ref, o_ref):
    c = cond_ref[...]
    a = (c * jax.nn.sigmoid(c)).astype(BF16)
    o_ref[...] = jnp.dot(a, w_ref[...].astype(BF16), preferred_element_type=F32) + b_ref[...]


def _adaln(cond8, w_mod, b_mod):
    depth, d, n6 = w_mod.shape
    tn = 512
    return pl.pallas_call(
        _adaln_body,
        grid=(depth, n6 // tn),
        in_specs=[pl.BlockSpec((8, d), lambda l, j: (0, 0)),
                  pl.BlockSpec((None, d, tn), lambda l, j: (l, 0, j)),
                  pl.BlockSpec((None, 1, tn), lambda l, j: (l, 0, j))],
        out_specs=pl.BlockSpec((None, 8, tn), lambda l, j: (l, 0, j)),
        out_shape=jax.ShapeDtypeStruct((depth, 8, n6), F32),
        compiler_params=_cp(("parallel", "parallel"), 40),
        name="adaln",
    )(cond8, w_mod, b_mod.reshape(depth, 1, n6))


def _prenorm_body(x_ref, ctx_ref, g_ref, mod_ref, xa_ref, h_ref):
    i = pl.program_id(0)

    def emit(v):
        xa_ref[...] = v
        n = _rms(v) * g_ref[0:1, :]
        h_ref[...] = (n * (1.0 + mod_ref[1:2, :]) + mod_ref[0:1, :]).astype(BF16)

    @pl.when(i == 0)
    def _():
        emit(ctx_ref[...])

    @pl.when(i > 0)
    def _():
        emit(x_ref[...])


def _prenorm(x2, ctx2, norm_g, mod4):
    seq, d = x2.shape
    nc = ctx2.shape[0]
    nt = seq + nc
    tr = ROW_TILE
    return pl.pallas_call(
        _prenorm_body,
        grid=(nt // tr,),
        in_specs=[pl.BlockSpec((tr, d), lambda i: (jnp.maximum(i - 1, 0), 0)),
                  pl.BlockSpec((tr, d), lambda i: (0, 0)),
                  pl.BlockSpec((None, 4, d), lambda i: (0, 0, 0)),
                  pl.BlockSpec((None, None, N_MOD, d), lambda i: (0, jnp.minimum(i, 1), 0, 0))],
        out_specs=[pl.BlockSpec((tr, d), lambda i: (i, 0)),
                   pl.BlockSpec((tr, d), lambda i: (i, 0))],
        out_shape=[jax.ShapeDtypeStruct((nt, d), F32), jax.ShapeDtypeStruct((nt, d), BF16)],
        compiler_params=_cp(("parallel",), 40),
        name="prenorm",
    )(x2, ctx2, norm_g, mod4)


def _resid_body(xa_ref, y_ref, gcur_ref, mcur_ref, gnext_ref, mnext_ref, xo_ref, h_ref, *,
                post_idx, gate_idx, next_g_idx, next_shift_idx):
    yn = _rms(y_ref[...].astype(F32)) * gcur_ref[post_idx:post_idx + 1, :]
    x = xa_ref[...] + mcur_ref[gate_idx:gate_idx + 1, :] * yn
    xo_ref[...] = x
    n = _rms(x) * gnext_ref[next_g_idx:next_g_idx + 1, :]
    h_ref[...] = (n * (1.0 + mnext_ref[next_shift_idx + 1:next_shift_idx + 2, :])
                  + mnext_ref[next_shift_idx:next_shift_idx + 1, :]).astype(BF16)


def _resid(xa, y, norm_g, mod4, layer, post_idx, gate_idx, next_layer, next_g_idx, next_shift_idx):
    nt, d = xa.shape
    tr = ROW_TILE
    row = lambda i: (i, 0)
    body = functools.partial(_resid_body, post_idx=post_idx, gate_idx=gate_idx,
                             next_g_idx=next_g_idx, next_shift_idx=next_shift_idx)
    return pl.pallas_call(
        body,
        grid=(nt // tr,),
        in_specs=[pl.BlockSpec((tr, d), row),
                  pl.BlockSpec((tr, d), row),
                  pl.BlockSpec((None, 4, d), lambda i: (layer, 0, 0)),
                  pl.BlockSpec((None, None, N_MOD, d), lambda i: (layer, jnp.minimum(i, 1), 0, 0)),
                  pl.BlockSpec((None, 4, d), lambda i: (next_layer, 0, 0)),
                  pl.BlockSpec((None, None, N_MOD, d), lambda i: (next_layer, jnp.minimum(i, 1), 0, 0))],
        out_specs=[pl.BlockSpec((tr, d), row), pl.BlockSpec((tr, d), row)],
        out_shape=[jax.ShapeDtypeStruct((nt, d), F32), jax.ShapeDtypeStruct((nt, d), BF16)],
        input_output_aliases={0: 0},
        compiler_params=_cp(("parallel",), 48),
        name="resid",
    )(xa, y, norm_g, mod4, norm_g, mod4)


def _final_body(xa_ref, y_ref, g_ref, m_ref, o_ref, *, post_idx, gate_idx):
    yn = _rms(y_ref[...].astype(F32)) * g_ref[post_idx:post_idx + 1, :]
    o_ref[...] = xa_ref[...] + m_ref[gate_idx:gate_idx + 1, :] * yn


def _final(xa, y, norm_g, mod4, layer, post_idx, gate_idx, n_ctx):
    nt, d = xa.shape
    tr = ROW_TILE
    off = n_ctx // tr
    body = functools.partial(_final_body, post_idx=post_idx, gate_idx=gate_idx)
    return pl.pallas_call(
        body,
        grid=((nt - n_ctx) // tr,),
        in_specs=[pl.BlockSpec((tr, d), lambda i: (i + off, 0)),
                  pl.BlockSpec((tr, d), lambda i: (i + off, 0)),
                  pl.BlockSpec((None, 4, d), lambda i: (layer, 0, 0)),
                  pl.BlockSpec((None, None, N_MOD, d), lambda i: (layer, 1, 0, 0))],
        out_specs=pl.BlockSpec((tr, d), lambda i: (i, 0)),
        out_shape=jax.ShapeDtypeStruct((nt - n_ctx, d), F32),
        compiler_params=_cp(("parallel",), 40),
        name="final",
    )(xa, y, norm_g, mod4)


def _inproj_body(h_ref, w_ref, cos_ref, sin_ref, gain_ref, scale_ref, o_ref, *, rope_lo, rope_hi, use_norm):
    j = pl.program_id(1)
    acc = jnp.dot(h_ref[...], w_ref[...], preferred_element_type=F32)
    is_rope = jnp.logical_and(j >= rope_lo, j < rope_hi)

    @pl.when(jnp.logical_not(is_rope))
    def _():
        o_ref[...] = acc.astype(BF16)

    @pl.when(is_rope)
    def _():
        c = cos_ref[...]
        s = sin_ref[...]
        lane = lax.broadcasted_iota(jnp.int32, c.shape, 1)
        first = (lane & 32) == 0
        for hh in range(acc.shape[1] // HEAD_DIM):
            sl = slice(hh * HEAD_DIM, (hh + 1) * HEAD_DIM)
            blk = acc[:, sl]
            if use_norm:
                blk = _rms(blk) * gain_ref[:, sl]
            rot = jnp.where(first, pltpu.roll(blk, HEAD_DIM - 32, 1), pltpu.roll(blk, 32, 1))
            o_ref[:, sl] = ((blk * c + rot * s) * scale_ref[:, sl]).astype(BF16)


def _inproj(h, w, cos_t, sin_t, gain, scale, rope_lo, rope_hi, use_norm):
    nt, d = h.shape
    n = w.shape[1]
    tm = _pick(nt, (1280, 640, 256))
    tn = 512
    body = functools.partial(_inproj_body, rope_lo=rope_lo // tn, rope_hi=rope_hi // tn, use_norm=use_norm)
    return pl.pallas_call(
        body,
        grid=(nt // tm, n // tn),
        in_specs=[pl.BlockSpec((tm, d), lambda i, j: (i, 0)),
                  pl.BlockSpec((d, tn), lambda i, j: (0, j)),
                  pl.BlockSpec((tm, HEAD_DIM), lambda i, j: (i, 0)),
                  pl.BlockSpec((tm, HEAD_DIM), lambda i, j: (i, 0)),
                  pl.BlockSpec((1, tn), lambda i, j: (0, j)),
                  pl.BlockSpec((1, tn), lambda i, j: (0, j))],
        out_specs=pl.BlockSpec((tm, tn), lambda i, j: (i, j)),
        out_shape=jax.ShapeDtypeStruct((nt, n), BF16),
        compiler_params=_cp(("parallel", "arbitrary"), 48),
        name="inproj",
    )(h, w, cos_t, sin_t, gain, scale)


def _mm_body(*refs, n_a):
    o_ref = refs[2 * n_a]
    acc = jnp.dot(refs[0][...], refs[n_a][...], preferred_element_type=F32)
    for t in range(1, n_a):
        acc += jnp.dot(refs[t][...], refs[n_a + t][...], preferred_element_type=F32)
    o_ref[...] = acc.astype(o_ref.dtype)


def _matmul(a_list, w_list, tm_cands, tn, out_dtype, vmem_mib, name):
    nt = a_list[0].shape[0]
    n = w_list[0].shape[1]
    tm = _pick(nt, tm_cands)
    in_specs = [pl.BlockSpec((tm, a.shape[1]), lambda i, j: (i, 0)) for a in a_list]
    in_specs += [pl.BlockSpec((w.shape[0], tn), lambda i, j: (0, j)) for w in w_list]
    return pl.pallas_call(
        functools.partial(_mm_body, n_a=len(a_list)),
        grid=(nt // tm, n // tn),
        in_specs=in_specs,
        out_specs=pl.BlockSpec((tm, tn), lambda i, j: (i, j)),
        out_shape=jax.ShapeDtypeStruct((nt, n), out_dtype),
        compiler_params=_cp(("parallel", "arbitrary"), vmem_mib),
        name=name,
    )(*a_list, *w_list)


def _swiglu_up_body(*refs, gated):
    if gated:
        h_ref, w1_ref, w3_ref, gate_ref, o_ref = refs
    else:
        h_ref, w1_ref, w3_ref, o_ref = refs
    h = h_ref[...]
    a = jnp.dot(h, w1_ref[...], preferred_element_type=F32)
    b = jnp.dot(h, w3_ref[...], preferred_element_type=F32)
    r = a * jax.nn.sigmoid(a) * b
    if gated:
        g = gate_ref[...]
        lane = lax.broadcasted_iota(jnp.int32, g.shape, 1)
        r = r * jnp.sum(jnp.where(lane == pl.program_id(1), g, 0.0), axis=-1, keepdims=True)
    o_ref[...] = r.astype(BF16)


def _swiglu_up(h, w1, w3, gates=None):
    nt, d = h.shape
    n_e, _, f = w1.shape
    gated = gates is not None
    if gated:
        tm, tn = _pick(nt, (640, 256)), f
        w_spec = pl.BlockSpec((None, d, tn), lambda i, j: (j, 0, 0))
    else:
        tm, tn = _pick(nt, (1280, 640, 256)), 512
        w_spec = pl.BlockSpec((None, d, tn), lambda i, j: (0, 0, j))
    in_specs = [pl.BlockSpec((tm, d), lambda i, j: (i, 0)), w_spec, w_spec]
    args = [h, w1, w3]
    if gated:
        in_specs.append(pl.BlockSpec((tm, LANES), lambda i, j: (i, 0)))
        args.append(gates)
    return pl.pallas_call(
        functools.partial(_swiglu_up_body, gated=gated),
        grid=(nt // tm, n_e * f // tn),
        in_specs=in_specs,
        out_specs=pl.BlockSpec((tm, tn), lambda i, j: (i, j)),
        out_shape=jax.ShapeDtypeStruct((nt, n_e * f), BF16),
        compiler_params=_cp(("parallel", "arbitrary"), 52),
        name="swiglu_up",
    )(*args)


def _router_body(h_ref, whi_ref, wlo_ref, g_ref, *, n_experts):
    h = h_ref[...]
    logits = (jnp.dot(h, whi_ref[...], preferred_element_type=F32)
              + jnp.dot(h, wlo_ref[...], preferred_element_type=F32))
    lane = lax.broadcasted_iota(jnp.int32, logits.shape, 1)
    lg = jnp.where(lane < n_experts, logits, NEG)
    v1 = jnp.max(lg, axis=-1, keepdims=True)
    i1 = jnp.min(jnp.where(lg == v1, lane, LANES), axis=-1, keepdims=True)
    lg2 = jnp.where(lane == i1, NEG, lg)
    v2 = jnp.max(lg2, axis=-1, keepdims=True)
    i2 = jnp.min(jnp.where(lg2 == v2, lane, LANES), axis=-1, keepdims=True)
    e = jnp.exp(v2 - v1)
    den = 1.0 + e
    g_ref[...] = jnp.where(lane == i1, 1.0 / den, 0.0) + jnp.where(lane == i2, e / den, 0.0)


def _router(h, router_w):
    nt, d = h.shape
    n_e = router_w.shape[1]
    whi = router_w.astype(BF16)
    wlo = (router_w - whi.astype(F32)).astype(BF16)
    pad = ((0, 0), (0, LANES - n_e))
    whi = jnp.pad(whi, pad)
    wlo = jnp.pad(wlo, pad)
    tm = _pick(nt, (640, 256))
    return pl.pallas_call(
        functools.partial(_router_body, n_experts=n_e),
        grid=(nt // tm,),
        in_specs=[pl.BlockSpec((tm, d), lambda i: (i, 0)),
                  pl.BlockSpec((d, LANES), lambda i: (0, 0)),
                  pl.BlockSpec((d, LANES), lambda i: (0, 0))],
        out_specs=pl.BlockSpec((tm, LANES), lambda i: (i, 0)),
        out_shape=jax.ShapeDtypeStruct((nt, LANES), F32),
        compiler_params=_cp(("parallel",), 32),
        name="router",
    )(h, whi, wlo)


V_ROWS = HEAD_DIM + 16
LOG2E = math.log2(math.e)


def _stack_heads(q, n_grp):
    return jnp.concatenate([q[:, g * HEAD_DIM:(g + 1) * HEAD_DIM] for g in range(n_grp)], axis=0)


def _unstack_store(o_ref, o, n_grp, rows):
    for g in range(n_grp):
        o_ref[:, g * HEAD_DIM:(g + 1) * HEAD_DIM] = o[g * rows:(g + 1) * rows, :].astype(o_ref.dtype)


def _scores_t(k, q):
    return lax.dot_general(k, q, (((1,), (1,)), ((), ())), preferred_element_type=F32)


def _transpose_v(v, n_kv):
    nt = v.shape[0]
    vt = v.reshape(nt, n_kv, HEAD_DIM).transpose(1, 2, 0)
    return jnp.concatenate([vt, jnp.ones((n_kv, V_ROWS - HEAD_DIM, nt), v.dtype)], axis=1)


def _win_body(sink_ref, q_ref, km_ref, k0_ref, kp_ref, kc_ref, vm_ref, v0_ref, vp_ref, vc_ref, o_ref, *,
              n_grp, ctx_blocks, n_lat):
    kvh = pl.program_id(0)
    n = pl.program_id(1)
    blk = q_ref.shape[0]
    qs = _stack_heads(q_ref[...], n_grp)
    keys = jnp.concatenate([km_ref[...], k0_ref[...], kp_ref[...], kc_ref[...]], axis=0)
    vt = jnp.concatenate([vm_ref[...], v0_ref[...], vp_ref[...], vc_ref[...]], axis=1)
    st = _scores_t(keys, qs)
    n_win = 3 * blk
    k_off = lax.broadcasted_iota(jnp.int32, st.shape, 0) - blk
    q_off = lax.broadcasted_iota(jnp.int32, st.shape, 1) % blk
    kpos = (n - ctx_blocks) * blk + k_off
    in_band = ((jnp.abs(q_off - k_off) <= WINDOW) & (kpos >= 0) & (kpos < n_lat) & (n >= ctx_blocks))
    st = jnp.where(in_band | (k_off >= n_win - blk), st, NEG)
    sink = jnp.concatenate(
        [jnp.full((1, blk), sink_ref[kvh * n_grp + g] * LOG2E, F32) for g in range(n_grp)], axis=1)
    m = jnp.maximum(jnp.max(st, axis=0, keepdims=True), sink)
    pt = jnp.exp2(st - m).astype(BF16)
    acc = jnp.dot(vt, pt, preferred_element_type=F32)
    den = acc[HEAD_DIM:HEAD_DIM + 1, :] + jnp.exp2(sink - m)
    _unstack_store(o_ref, (acc[0:HEAD_DIM, :] / den).T, n_grp, blk)


def _window_attention(proj, sink, n_ctx, q_col, k_col, v_col, n_kv, n_grp):
    nt = proj.shape[0]
    blk = WINDOW
    nb = nt // blk
    cb = n_ctx // blk
    qc, kc = q_col // (n_grp * HEAD_DIM), k_col // HEAD_DIM
    vt = _transpose_v(proj[:, v_col:v_col + n_kv * HEAD_DIM], n_kv)
    lo = lambda n: jnp.maximum(n - 1, cb)
    hi = lambda n: jnp.minimum(n + 1, nb - 1)
    same = lambda n: n
    k_tile = lambda rowf: pl.BlockSpec((blk, HEAD_DIM), lambda h, n: (rowf(n), kc + h))
    v_tile = lambda rowf: pl.BlockSpec((None, V_ROWS, blk), lambda h, n: (h, 0, rowf(n)))
    body = functools.partial(_win_body, n_grp=n_grp, ctx_blocks=cb, n_lat=nt - n_ctx)
    return pl.pallas_call(
        body,
        grid=(n_kv, nb),
        in_specs=[pl.BlockSpec(memory_space=pltpu.SMEM),
                  pl.BlockSpec((blk, n_grp * HEAD_DIM), lambda h, n: (n, qc + h)),
                  k_tile(lo), k_tile(same), k_tile(hi),
                  pl.BlockSpec((n_ctx, HEAD_DIM), lambda h, n: (0, kc + h)),
                  v_tile(lo), v_tile(same), v_tile(hi),
                  pl.BlockSpec((None, V_ROWS, n_ctx), lambda h, n: (h, 0, 0))],
        out_specs=pl.BlockSpec((blk, n_grp * HEAD_DIM), lambda h, n: (n, h)),
        out_shape=jax.ShapeDtypeStruct((nt, n_kv * n_grp * HEAD_DIM), BF16),
        compiler_params=_cp(("parallel", "parallel"), 32),
        name="window_attn",
    )(sink, proj, proj, proj, proj, proj, vt, vt, vt, vt)


def _flash_body(q_ref, k_ref, vt_ref, o_ref, acc_ref, *, n_grp, tk, n_chunks, n_ctx):
    qi = pl.program_id(1)
    tq = q_ref.shape[0]
    qs = _stack_heads(q_ref[...], n_grp)
    is_ctx = qi == 0

    def chunk(c, limit, m_old):
        start = c * tk if isinstance(c, int) else pl.multiple_of(c * tk, tk)
        k = k_ref[pl.ds(start, tk), :]
        st = _scores_t(k, qs)
        if limit is not None:
            row = lax.broadcasted_iota(jnp.int32, st.shape, 0)
            st = jnp.where(row < limit, st, NEG)
        m_new = jnp.maximum(m_old, jnp.max(st, axis=0, keepdims=True))
        alpha = jnp.exp2(m_old - m_new)
        pt = jnp.exp2(st - m_new).astype(BF16)
        acc_ref[...] = alpha * acc_ref[...] + jnp.dot(vt_ref[c], pt, preferred_element_type=F32)
        return m_new

    acc_ref[...] = jnp.zeros(acc_ref.shape, F32)
    m = chunk(0, jnp.where(is_ctx, n_ctx, tk), jnp.full((1, qs.shape[0]), NEG, F32))
    lax.fori_loop(1, jnp.where(is_ctx, 1, n_chunks), lambda c, m_old: chunk(c, None, m_old), m)
    acc = acc_ref[...]
    _unstack_store(o_ref, (acc[0:HEAD_DIM, :] / acc[HEAD_DIM:HEAD_DIM + 1, :]).T, n_grp, tq)


def _flash_attention(proj, n_ctx, q_col, k_col, v_col, n_kv, n_grp):
    nt = proj.shape[0]
    tq = n_ctx
    tk = _pick(nt, (1280, 256))
    n_chunks = nt // tk
    qc, kc = q_col // (n_grp * HEAD_DIM), k_col // HEAD_DIM
    vt = _transpose_v(proj[:, v_col:v_col + n_kv * HEAD_DIM], n_kv)
    vt = vt.reshape(n_kv, V_ROWS, n_chunks, tk).transpose(0, 2, 1, 3)
    body = functools.partial(_flash_body, n_grp=n_grp, tk=tk, n_chunks=n_chunks, n_ctx=n_ctx)
    return pl.pallas_call(
        body,
        grid=(n_kv, nt // tq),
        in_specs=[pl.BlockSpec((tq, n_grp * HEAD_DIM), lambda h, i: (i, qc + h)),
                  pl.BlockSpec((nt, HEAD_DIM), lambda h, i: (0, kc + h)),
                  pl.BlockSpec((None, n_chunks, V_ROWS, tk), lambda h, i: (h, 0, 0, 0))],
        out_specs=pl.BlockSpec((tq, n_grp * HEAD_DIM), lambda h, i: (i, h)),
        out_shape=jax.ShapeDtypeStruct((nt, n_kv * n_grp * HEAD_DIM), BF16),
        scratch_shapes=[pltpu.VMEM((V_ROWS, n_grp * tq), F32)],
        compiler_params=_cp(("parallel", "parallel"), 48),
        name="flash_attn",
    )(proj, proj, vt)


def _s5_weights(lam_re, lam_im, log_dt, b_re, b_im, c_re, c_im, d_skip):
    hp = lax.Precision.HIGHEST
    n_g, n_p = lam_re.shape[1], lam_re.shape[2]
    n_h = b_re.shape[-1]
    ell = S5_CHUNK
    dt = jnp.exp(log_dt)[..., None]
    ar, ai = lam_re * dt, lam_im * dt
    k = jnp.arange(ell + 1, dtype=F32)[:, None, None, None]
    mag = jnp.exp(k * ar[None])
    lp_re, lp_im = mag * jnp.cos(k * ai[None]), mag * jnp.sin(k * ai[None])
    nr, ni = lp_re[1] - 1.0, lp_im[1]
    den = lam_re * lam_re + lam_im * lam_im
    fr, fi = (nr * lam_re + ni * lam_im) / den, (ni * lam_re - nr * lam_im) / den
    bb_re = fr[..., None] * b_re - fi[..., None] * b_im
    bb_im = fr[..., None] * b_im + fi[..., None] * b_re
    lpe_re, lpe_im = lp_re[:, :, :, None, :], lp_im[:, :, :, None, :]
    cl_re = c_re[None] * lpe_re - c_im[None] * lpe_im
    cl_im = c_re[None] * lpe_im + c_im[None] * lpe_re
    m = (jnp.einsum('kdghp,dgpi->kdghi', cl_re, bb_re, precision=hp)
         - jnp.einsum('kdghp,dgpi->kdghi', cl_im, bb_im, precision=hp))
    t_in = jnp.arange(ell)[:, None]
    t_out = jnp.arange(ell)[None, :]
    lag = t_out - t_in
    sel = lambda cond: cond[:, :, None, None, None]
    toep = (jnp.where(sel(lag >= 0), m[:, 0][jnp.clip(lag, 0, ell)], 0.0)
            + jnp.where(sel(lag <= 0), m[:, 1][jnp.clip(-lag, 0, ell)], 0.0))
    toep = toep.transpose(2, 0, 4, 1, 3).reshape(n_g // 2, 2, ell * n_h, ell * n_h)
    eye2 = jnp.eye(2, dtype=F32)
    toep2 = jnp.einsum('pgab,gh->pgahb', toep, eye2).reshape(n_g // 2, 2 * ell * n_h, 2 * ell * n_h)

    def state_in(direction, powers):
        pr = lp_re[powers, direction].transpose(1, 0, 2)[:, :, None, :]
        pi = lp_im[powers, direction].transpose(1, 0, 2)[:, :, None, :]
        br = bb_re[direction].transpose(0, 2, 1)[:, None, :, :]
        bi = bb_im[direction].transpose(0, 2, 1)[:, None, :, :]
        return pr * br - pi * bi, pr * bi + pi * br

    steps = jnp.arange(ell)
    f_re, f_im = state_in(0, ell - 1 - steps)
    g_re, g_im = state_in(1, steps)
    w_in = jnp.stack([f_re, f_im, g_re, g_im], axis=3)
    w_in = w_in.reshape(n_g // 2, 2, ell * n_h, 4, n_p)
    w_in2 = jnp.einsum('pgakn,gh->pgakhn', w_in, eye2).reshape(n_g // 2, 2 * ell * n_h, 4 * 2 * n_p)

    def state_out(direction, powers):
        wr = cl_re[powers, direction].transpose(1, 3, 0, 2)
        wi = -cl_im[powers, direction].transpose(1, 3, 0, 2)
        return jnp.stack([wr, wi], axis=1)

    def pair_rows(w):
        w = w.reshape(n_g // 2, 2, 2, n_p, ell * n_h)
        return jnp.einsum('pgkna,gh->pkgnha', w, eye2).reshape(n_g // 2, 2 * 2 * n_p, 2 * ell * n_h)

    w_of = pair_rows(state_out(0, steps + 1))
    w_ob = pair_rows(state_out(1, ell - steps))
    lam_l = jnp.stack([lp_re[ell, 0], lp_im[ell, 0], lp_re[ell, 1], lp_im[ell, 1]], axis=0)
    lam_l = lam_l.reshape(4, n_g // 2, 2 * n_p)
    d2 = jnp.broadcast_to(d_skip.reshape(n_g // 2, 2, 1, n_h), (n_g // 2, 2, ell, n_h))
    d2 = d2.reshape(n_g // 2, 1, 2 * ell * n_h)
    return toep2.astype(BF16), w_in2.astype(BF16), w_of.astype(BF16), w_ob.astype(BF16), lam_l, d2


def _s5_in_body(u_ref, w_ref, x_ref):
    x_ref[...] = jnp.dot(u_ref[...], w_ref[...], preferred_element_type=F32)


def _s5_scan_body(xf_ref, xb_ref, lam_ref, sf_ref, sb_ref, st_ref):
    @pl.when(pl.program_id(0) == 0)
    def _():
        st_ref[...] = jnp.zeros(st_ref.shape, F32)

    half = xf_ref.shape[2] // 2
    rows = xf_ref.shape[0]

    def run(x_ref, s_ref, base, order):
        ar, ai = lam_ref[base], lam_ref[base + 1]
        sr, si = st_ref[base], st_ref[base + 1]
        for r in order:
            s_ref[r, :, 0:half] = sr
            s_ref[r, :, half:2 * half] = si
            xr, xi = x_ref[r, :, 0:half], x_ref[r, :, half:2 * half]
            sr, si = ar * sr - ai * si + xr, ar * si + ai * sr + xi
        st_ref[base] = sr
        st_ref[base + 1] = si

    run(xf_ref, sf_ref, 0, range(rows))
    run(xb_ref, sb_ref, 2, range(rows - 1, -1, -1))


def _s5_out_body(u_ref, sf_ref, sb_ref, toep_ref, wof_ref, wob_ref, d_ref, y_ref):
    u = u_ref[...]
    y = jnp.dot(u, toep_ref[...], preferred_element_type=F32)
    y += jnp.dot(sf_ref[...].astype(BF16), wof_ref[...], preferred_element_type=F32)
    y += jnp.dot(sb_ref[...].astype(BF16), wob_ref[...], preferred_element_type=F32)
    y_ref[...] = y + u.astype(F32) * d_ref[...]


def _s5(u, n_ctx, weights):
    toep2, w_in2, w_of, w_ob, lam_l, d2 = weights
    nt, width = u.shape
    ell = S5_CHUNK
    n_pair = toep2.shape[0]
    kdim = toep2.shape[1]
    sdim = w_in2.shape[2]
    n_chunk = nt // ell
    h = S5_GROUP
    u2 = u.reshape(n_chunk, ell, n_pair, 2, h).transpose(2, 0, 3, 1, 4).reshape(n_pair, n_chunk, kdim)
    per_pair = lambda r, c: pl.BlockSpec((None, r, c), lambda p: (p, 0, 0))
    x = pl.pallas_call(
        _s5_in_body,
        grid=(n_pair,),
        in_specs=[per_pair(n_chunk, kdim), per_pair(kdim, sdim)],
        out_specs=per_pair(n_chunk, sdim),
        out_shape=jax.ShapeDtypeStruct((n_pair, n_chunk, sdim), F32),
        compiler_params=_cp(("parallel",), 32),
        name="s5_state_in",
    )(u2, w_in2)
    xt = x.transpose(1, 0, 2)
    rb = n_ctx // ell
    nb = n_chunk // rb
    half = sdim // 2
    bwd = lambda b: (jnp.where(b == 0, 0, nb - b), 0, 0)
    sf, sb = pl.pallas_call(
        _s5_scan_body,
        grid=(nb,),
        in_specs=[pl.BlockSpec((rb, n_pair, half), lambda b: (b, 0, 0)),
                  pl.BlockSpec((rb, n_pair, half), lambda b: (jnp.where(b == 0, 0, nb - b), 0, 1)),
                  pl.BlockSpec((4, n_pair, half // 2), lambda b: (0, 0, 0))],
        out_specs=[pl.BlockSpec((rb, n_pair, half), lambda b: (b, 0, 0)),
                   pl.BlockSpec((rb, n_pair, half), bwd)],
        out_shape=[jax.ShapeDtypeStruct((n_chunk, n_pair, half), F32),
                   jax.ShapeDtypeStruct((n_chunk, n_pair, half), F32)],
        scratch_shapes=[pltpu.VMEM((4, n_pair, half // 2), F32)],
        compiler_params=_cp(("arbitrary",), 32),
        name="s5_scan",
    )(xt, xt, lam_l)
    y2 = pl.pallas_call(
        _s5_out_body,
        grid=(n_pair,),
        in_specs=[per_pair(n_chunk, kdim), per_pair(n_chunk, half), per_pair(n_chunk, half),
                  per_pair(kdim, kdim), per_pair(half, kdim), per_pair(half, kdim), per_pair(1, kdim)],
        out_specs=per_pair(n_chunk, kdim),
        out_shape=jax.ShapeDtypeStruct((n_pair, n_chunk, kdim), F32),
        compiler_params=_cp(("parallel",), 32),
        name="s5_out",
    )(u2, sf.transpose(1, 0, 2), sb.transpose(1, 0, 2), toep2, w_of, w_ob, d2)
    return y2.reshape(n_pair, n_chunk, 2, ell, h).transpose(1, 3, 0, 2, 4).reshape(nt, width)


def _glu_body(y_ref, w_ref, b_ref, o_ref):
    y = y_ref[...]
    gy = 0.5 * y * (1.0 + jnp.tanh(math.sqrt(2.0 / math.pi) * (y + 0.044715 * (y * y * y))))
    z = jnp.dot(gy.astype(BF16), w_ref[...], preferred_element_type=F32) + b_ref[...]
    o_ref[...] = (gy * jax.nn.sigmoid(z)).astype(BF16)


def _s5_glu(y, glu_w, glu_b):
    nt, width = y.shape
    tm = _pick(nt, (1280, 640, 256))
    return pl.pallas_call(
        _glu_body,
        grid=(nt // tm,),
        in_specs=[pl.BlockSpec((tm, width), lambda i: (i, 0)),
                  pl.BlockSpec((width, width), lambda i: (0, 0)),
                  pl.BlockSpec((1, width), lambda i: (0, 0))],
        out_specs=pl.BlockSpec((tm, width), lambda i: (i, 0)),
        out_shape=jax.ShapeDtypeStruct((nt, width), BF16),
        compiler_params=_cp(("parallel",), 40),
        name="s5_glu",
    )(y, glu_w, glu_b.reshape(1, width))


def _rope_tables(seq, n_ctx):
    t = jnp.arange(seq)
    row_id = (t // GRID_W).astype(F32)
    col_id = (t % GRID_W).astype(F32)
    n_freq = HEAD_DIM // 4
    inv = ROPE_THETA ** (-jnp.arange(n_freq, dtype=F32) / n_freq)
    ang_r = row_id[:, None] * inv[None, :]
    ang_c = col_id[:, None] * inv[None, :]
    cos_t = jnp.concatenate([jnp.cos(ang_r), jnp.cos(ang_r), jnp.cos(ang_c), jnp.cos(ang_c)], axis=1)
    sin_t = jnp.concatenate([-jnp.sin(ang_r), jnp.sin(ang_r), -jnp.sin(ang_c), jnp.sin(ang_c)], axis=1)
    cos_t = jnp.concatenate([jnp.ones((n_ctx, HEAD_DIM), F32), cos_t], axis=0)
    sin_t = jnp.concatenate([jnp.zeros((n_ctx, HEAD_DIM), F32), sin_t], axis=0)
    return cos_t, sin_t


def kernel(x, c, ctx, c_ctx, w_mod, b_mod, norm_g, ab_w_in, ab_w_out, s5_lam_re, s5_lam_im, s5_log_dt, s5_b_re, s5_b_im, s5_c_re, s5_c_im, s5_d, glu_w, glu_b, attn_sink, ffn_w1, ffn_w3, ffn_w2, c_w_in, c_w_out, q_norm, k_norm, router_w, moe_w1, moe_w3, moe_w2):
    assert x.shape[0] == 1 and ctx.shape[0] == 1
    seq, d = x.shape[1], x.shape[2]
    n_ctx = ctx.shape[1]
    assert n_ctx == ROW_TILE
    depth = w_mod.shape[0]
    s5_width = s5_d.shape[1] * s5_d.shape[2]
    b_heads = attn_sink.shape[1]
    b_kv = (ab_w_in.shape[2] - s5_width - b_heads * HEAD_DIM) // (2 * HEAD_DIM)
    c_heads = c_w_out.shape[1] // HEAD_DIM
    c_kv = (c_w_in.shape[2] - c_heads * HEAD_DIM) // (2 * HEAD_DIM)
    q_scale = HEAD_DIM ** -0.5 * LOG2E

    cond8 = jnp.zeros((8, d), F32).at[0].set(c_ctx).at[1].set(c[0])
    mod4 = _adaln(cond8, w_mod, b_mod).reshape(depth, 8, N_MOD, d)
    cos_t, sin_t = _rope_tables(seq, n_ctx)
    xa, h = _prenorm(x[0], ctx[0], norm_g, mod4)

    def col_vec(pieces):
        return jnp.concatenate([jnp.broadcast_to(jnp.asarray(v, F32), (n,)) for v, n in pieces]).reshape(1, -1)

    for i in range(depth):
        j = i // 2
        if i % 2 == 0:
            q0 = s5_width
            k0 = q0 + b_heads * HEAD_DIM
            v0 = k0 + b_kv * HEAD_DIM
            n_in = v0 + b_kv * HEAD_DIM
            proj = _inproj(h, ab_w_in[j].astype(BF16), cos_t, sin_t,
                           jnp.ones((1, n_in), F32),
                           col_vec([(1.0, q0), (q_scale, k0 - q0), (1.0, n_in - k0)]),
                           q0, v0, False)
            s5w = _s5_weights(s5_lam_re[j], s5_lam_im[j], s5_log_dt[j], s5_b_re[j], s5_b_im[j],
                              s5_c_re[j], s5_c_im[j], s5_d[j])
            a = _s5_glu(_s5(proj[:, :s5_width], n_ctx, s5w), glu_w[j].astype(BF16), glu_b[j])
            o = _window_attention(proj, attn_sink[j], n_ctx, q0, k0, v0, b_kv, b_heads // b_kv)
            w_out = ab_w_out[j].astype(BF16)
            y = _matmul([a, o], [w_out[:s5_width], w_out[s5_width:]], (1280, 640, 256), 512, F32, 48, "ab_out")
        else:
            k0 = c_heads * HEAD_DIM
            v0 = k0 + c_kv * HEAD_DIM
            n_in = v0 + c_kv * HEAD_DIM
            proj = _inproj(h, c_w_in[j].astype(BF16), cos_t, sin_t,
                           col_vec([(jnp.tile(q_norm[j], c_heads), k0), (jnp.tile(k_norm[j], c_kv), v0 - k0),
                                    (1.0, n_in - v0)]),
                           col_vec([(q_scale, k0), (1.0, n_in - k0)]),
                           0, v0, True)
            o = _flash_attention(proj, n_ctx, 0, k0, v0, c_kv, c_heads // c_kv)
            y = _matmul([o], [c_w_out[j].astype(BF16)], (1280, 640, 256), 512, F32, 48, "c_out")
        xa, h = _resid(xa, y, norm_g, mod4, i, 1, 2, i, 2, 3)
        if i % 2 == 0:
            g = _swiglu_up(h, ffn_w1[j].astype(BF16)[None], ffn_w3[j].astype(BF16)[None])
            f = _matmul([g], [ffn_w2[j].astype(BF16)], (1280, 640, 256), 512, F32, 56, "ffn_down")
        else:
            gates = _router(h, router_w[j])
            g = _swiglu_up(h, moe_w1[j].astype(BF16), moe_w3[j].astype(BF16), gates)
            w2 = moe_w2[j].astype(BF16).reshape(-1, d)
            f = _matmul([g], [w2], (640, 256), 512, F32, 56, "moe_down")
        if i + 1 < depth:
            xa, h = _resid(xa, f, norm_g, mod4, i, 3, 5, i + 1, 0, 0)
        else:
            out = _final(xa, f, norm_g, mod4, i, 3, 5, n_ctx)
    return out[None]
```

```python
import functools
import math

import jax
import jax.numpy as jnp
from jax import lax
from jax.experimental import pallas as pl
from jax.experimental.pallas import tpu as pltpu

F32 = jnp.float32
BF16 = jnp.bfloat16

HEAD_DIM = 128
GRID_W = 64
ROPE_THETA = 10000.0
NORM_EPS = 1e-6
NEG = -1e30
N_MOD = 6
S5_GROUP = 16
S5_STATE = 64
S5_CHUNK = 16
WINDOW = 128
TOP_K = 2
LANES = 128
ROW_TILE = 256
MIB = 1024 * 1024


def _cp(sem, vmem_mib):
    return pltpu.CompilerParams(dimension_semantics=sem, vmem_limit_bytes=vmem_mib * MIB)


def _pick(n, cands):
    for t in cands:
        if n % t == 0:
            return t
    raise ValueError(f"no tile for {n} in {cands}")


def _rms(v):
    return v * lax.rsqrt(jnp.mean(v * v, axis=-1, keepdims=True) + NORM_EPS)


def _adaln_body(cond_ref, w_ref, b_ref, o_ref):
    c = cond_ref[...]
    a = (c * jax.nn.sigmoid(c)).astype(BF16)
    o_ref[...] = jnp.dot(a, w_ref[...].astype(BF16), preferred_element_type=F32) + b_ref[...]


def _adaln(cond8, w_mod, b_mod):
    depth, d, n6 = w_mod.shape
    tn = 512
    return pl.pallas_call(
        _adaln_body,
        grid=(depth, n6 // tn),
        in_specs=[pl.BlockSpec((8, d), lambda l, j: (0, 0)),
                  pl.BlockSpec((None, d, tn), lambda l, j: (l, 0, j)),
                  pl.BlockSpec((None, 1, tn), lambda l, j: (l, 0, j))],
        out_specs=pl.BlockSpec((None, 8, tn), lambda l, j: (l, 0, j)),
        out_shape=jax.ShapeDtypeStruct((depth, 8, n6), F32),
        compiler_params=_cp(("parallel", "parallel"), 40),
        name="adaln",
    )(cond8, w_mod, b_mod.reshape(depth, 1, n6))


def _prenorm_body(x_ref, ctx_ref, g_ref, mod_ref, xa_ref, h_ref):
    i = pl.program_id(0)

    def emit(v):
        xa_ref[...] = v
        n = _rms(v) * g_ref[0:1, :]
        h_ref[...] = (n * (1.0 + mod_ref[1:2, :]) + mod_ref[0:1, :]).astype(BF16)

    @pl.when(i == 0)
    def _():
        emit(ctx_ref[...])

    @pl.when(i > 0)
    def _():
        emit(x_ref[...])


def _prenorm(x2, ctx2, norm_g, mod4):
    seq, d = x2.shape
    nc = ctx2.shape[0]
    nt = seq + nc
    tr = ROW_TILE
    return pl.pallas_call(
        _prenorm_body,
        grid=(nt // tr,),
        in_specs=[pl.BlockSpec((tr, d), lambda i: (jnp.maximum(i - 1, 0), 0)),
                  pl.BlockSpec((tr, d), lambda i: (0, 0)),
                  pl.BlockSpec((None, 4, d), lambda i: (0, 0, 0)),
                  pl.BlockSpec((None, None, N_MOD, d), lambda i: (0, jnp.minimum(i, 1), 0, 0))],
        out_specs=[pl.BlockSpec((tr, d), lambda i: (i, 0)),
                   pl.BlockSpec((tr, d), lambda i: (i, 0))],
        out_shape=[jax.ShapeDtypeStruct((nt, d), F32), jax.ShapeDtypeStruct((nt, d), BF16)],
        compiler_params=_cp(("parallel",), 40),
        name="prenorm",
    )(x2, ctx2, norm_g, mod4)


def _resid_body(xa_ref, y_ref, gcur_ref, mcur_ref, gnext_ref, mnext_ref, xo_ref, h_ref, *,
                post_idx, gate_idx, next_g_idx, next_shift_idx):
    yn = _rms(y_ref[...].astype(F32)) * gcur_ref[post_idx:post_idx + 1, :]
    x = xa_ref[...] + mcur_ref[gate_idx:gate_idx + 1, :] * yn
    xo_ref[...] = x
    n = _rms(x) * gnext_ref[next_g_idx:next_g_idx + 1, :]
    h_ref[...] = (n * (1.0 + mnext_ref[next_shift_idx + 1:next_shift_idx + 2, :])
                  + mnext_ref[next_shift_idx:next_shift_idx + 1, :]).astype(BF16)


def _resid(xa, y, norm_g, mod4, layer, post_idx, gate_idx, next_layer, next_g_idx, next_shift_idx):
    nt, d = xa.shape
    tr = ROW_TILE
    row = lambda i: (i, 0)
    body = functools.partial(_resid_body, post_idx=post_idx, gate_idx=gate_idx,
                             next_g_idx=next_g_idx, next_shift_idx=next_shift_idx)
    return pl.pallas_call(
        body,
        grid=(nt // tr,),
        in_specs=[pl.BlockSpec((tr, d), row),
                  pl.BlockSpec((tr, d), row),
                  pl.BlockSpec((None, 4, d), lambda i: (layer, 0, 0)),
                  pl.BlockSpec((None, None, N_MOD, d), lambda i: (layer, jnp.minimum(i, 1), 0, 0)),
                  pl.BlockSpec((None, 4, d), lambda i: (next_layer, 0, 0)),
                  pl.BlockSpec((None, None, N_MOD, d), lambda i: (next_layer, jnp.minimum(i, 1), 0, 0))],
        out_specs=[pl.BlockSpec((tr, d), row), pl.BlockSpec((tr, d), row)],
        out_shape=[jax.ShapeDtypeStruct((nt, d), F32), jax.ShapeDtypeStruct((nt, d), BF16)],
        input_output_aliases={0: 0},
        compiler_params=_cp(("parallel",), 48),
        name="resid",
    )(xa, y, norm_g, mod4, norm_g, mod4)


def _final_body(xa_ref, y_ref, g_ref, m_ref, o_ref, *, post_idx, gate_idx):
    yn = _rms(y_ref[...].astype(F32)) * g_ref[post_idx:post_idx + 1, :]
    o_ref[...] = xa_ref[...] + m_ref[gate_idx:gate_idx + 1, :] * yn


def _final(xa, y, norm_g, mod4, layer, post_idx, gate_idx, n_ctx):
    nt, d = xa.shape
    tr = ROW_TILE
    off = n_ctx // tr
    body = functools.partial(_final_body, post_idx=post_idx, gate_idx=gate_idx)
    return pl.pallas_call(
        body,
        grid=((nt - n_ctx) // tr,),
        in_specs=[pl.BlockSpec((tr, d), lambda i: (i + off, 0)),
                  pl.BlockSpec((tr, d), lambda i: (i + off, 0)),
                  pl.BlockSpec((None, 4, d), lambda i: (layer, 0, 0)),
                  pl.BlockSpec((None, None, N_MOD, d), lambda i: (layer, 1, 0, 0))],
        out_specs=pl.BlockSpec((tr, d), lambda i: (i, 0)),
        out_shape=jax.ShapeDtypeStruct((nt - n_ctx, d), F32),
        compiler_params=_cp(("parallel",), 40),
        name="final",
    )(xa, y, norm_g, mod4)


def _inproj_body(h_ref, w_ref, cos_ref, sin_ref, gain_ref, scale_ref, o_ref, *, rope_lo, rope_hi, use_norm):
    j = pl.program_id(1)
    acc = jnp.dot(h_ref[...], w_ref[...], preferred_element_type=F32)
    is_rope = jnp.logical_and(j >= rope_lo, j < rope_hi)

    @pl.when(jnp.logical_not(is_rope))
    def _():
        o_ref[...] = acc.astype(BF16)

    @pl.when(is_rope)
    def _():
        c = cos_ref[...]
        s = sin_ref[...]
        lane = lax.broadcasted_iota(jnp.int32, c.shape, 1)
        first = (lane & 32) == 0
        for hh in range(acc.shape[1] // HEAD_DIM):
            sl = slice(hh * HEAD_DIM, (hh + 1) * HEAD_DIM)
            blk = acc[:, sl]
            if use_norm:
                blk = _rms(blk) * gain_ref[:, sl]
            rot = jnp.where(first, pltpu.roll(blk, HEAD_DIM - 32, 1), pltpu.roll(blk, 32, 1))
            o_ref[:, sl] = ((blk * c + rot * s) * scale_ref[:, sl]).astype(BF16)


def _inproj(h, w, cos_t, sin_t, gain, scale, rope_lo, rope_hi, use_norm):
    nt, d = h.shape
    n = w.shape[1]
    tm = _pick(nt, (1280, 640, 256))
    tn = 512
    body = functools.partial(_inproj_body, rope_lo=rope_lo // tn, rope_hi=rope_hi // tn, use_norm=use_norm)
    return pl.pallas_call(
        body,
        grid=(nt // tm, n // tn),
        in_specs=[pl.BlockSpec((tm, d), lambda i, j: (i, 0)),
                  pl.BlockSpec((d, tn), lambda i, j: (0, j)),
                  pl.BlockSpec((tm, HEAD_DIM), lambda i, j: (i, 0)),
                  pl.BlockSpec((tm, HEAD_DIM), lambda i, j: (i, 0)),
                  pl.BlockSpec((1, tn), lambda i, j: (0, j)),
                  pl.BlockSpec((1, tn), lambda i, j: (0, j))],
        out_specs=pl.BlockSpec((tm, tn), lambda i, j: (i, j)),
        out_shape=jax.ShapeDtypeStruct((nt, n), BF16),
        compiler_params=_cp(("parallel", "arbitrary"), 48),
        name="inproj",
    )(h, w, cos_t, sin_t, gain, scale)


def _mm_body(*refs, n_a):
    o_ref = refs[2 * n_a]
    acc = jnp.dot(refs[0][...], refs[n_a][...], preferred_element_type=F32)
    for t in range(1, n_a):
        acc += jnp.dot(refs[t][...], refs[n_a + t][...], preferred_element_type=F32)
    o_ref[...] = acc.astype(o_ref.dtype)


def _matmul(a_list, w_list, tm_cands, tn, out_dtype, vmem_mib, name):
    nt = a_list[0].shape[0]
    n = w_list[0].shape[1]
    tm = _pick(nt, tm_cands)
    in_specs = [pl.BlockSpec((tm, a.shape[1]), lambda i, j: (i, 0)) for a in a_list]
    in_specs += [pl.BlockSpec((w.shape[0], tn), lambda i, j: (0, j)) for w in w_list]
    return pl.pallas_call(
        functools.partial(_mm_body, n_a=len(a_list)),
        grid=(nt // tm, n // tn),
        in_specs=in_specs,
        out_specs=pl.BlockSpec((tm, tn), lambda i, j: (i, j)),
        out_shape=jax.ShapeDtypeStruct((nt, n), out_dtype),
        compiler_params=_cp(("parallel", "arbitrary"), vmem_mib),
        name=name,
    )(*a_list, *w_list)


def _swiglu_up_body(*refs, gated):
    if gated:
        h_ref, w1_ref, w3_ref, gate_ref, o_ref = refs
    else:
        h_ref, w1_ref, w3_ref, o_ref = refs
    h = h_ref[...]
    a = jnp.dot(h, w1_ref[...], preferred_element_type=F32)
    b = jnp.dot(h, w3_ref[...], preferred_element_type=F32)
    r = a * jax.nn.sigmoid(a) * b
    if gated:
        g = gate_ref[...]
        lane = lax.broadcasted_iota(jnp.int32, g.shape, 1)
        r = r * jnp.sum(jnp.where(lane == pl.program_id(1), g, 0.0), axis=-1, keepdims=True)
    o_ref[...] = r.astype(BF16)


def _swiglu_up(h, w1, w3, gates=None):
    nt, d = h.shape
    n_e, _, f = w1.shape
    gated = gates is not None
    if gated:
        tm, tn = _pick(nt, (640, 256)), f
        w_spec = pl.BlockSpec((None, d, tn), lambda i, j: (j, 0, 0))
    else:
        tm, tn = _pick(nt, (1280, 640, 256)), 512
        w_spec = pl.BlockSpec((None, d, tn), lambda i, j: (0, 0, j))
    in_specs = [pl.BlockSpec((tm, d), lambda i, j: (i, 0)), w_spec, w_spec]
    args = [h, w1, w3]
    if gated:
        in_specs.append(pl.BlockSpec((tm, LANES), lambda i, j: (i, 0)))
        args.append(gates)
    return pl.pallas_call(
        functools.partial(_swiglu_up_body, gated=gated),
        grid=(nt // tm, n_e * f // tn),
        in_specs=in_specs,
        out_specs=pl.BlockSpec((tm, tn), lambda i, j: (i, j)),
        out_shape=jax.ShapeDtypeStruct((nt, n_e * f), BF16),
        compiler_params=_cp(("parallel", "arbitrary"), 52),
        name="swiglu_up",
    )(*args)


def _router_body(h_ref, whi_ref, wlo_ref, g_ref, *, n_experts):
    h = h_ref[...]
    logits = (jnp.dot(h, whi_ref[...], preferred_element_type=F32)
              + jnp.dot(h, wlo_ref[...], preferred_element_type=F32))
    lane = lax.broadcasted_iota(jnp.int32, logits.shape, 1)
    lg = jnp.where(lane < n_experts, logits, NEG)
    v1 = jnp.max(lg, axis=-1, keepdims=True)
    i1 = jnp.min(jnp.where(lg == v1, lane, LANES), axis=-1, keepdims=True)
    lg2 = jnp.where(lane == i1, NEG, lg)
    v2 = jnp.max(lg2, axis=-1, keepdims=True)
    i2 = jnp.min(jnp.where(lg2 == v2, lane, LANES), axis=-1, keepdims=True)
    e = jnp.exp(v2 - v1)
    den = 1.0 + e
    g_ref[...] = jnp.where(lane == i1, 1.0 / den, 0.0) + jnp.where(lane == i2, e / den, 0.0)


def _router(h, router_w):
    nt, d = h.shape
    n_e = router_w.shape[1]
    whi = router_w.astype(BF16)
    wlo = (router_w - whi.astype(F32)).astype(BF16)
    pad = ((0, 0), (0, LANES - n_e))
    whi = jnp.pad(whi, pad)
    wlo = jnp.pad(wlo, pad)
    tm = _pick(nt, (640, 256))
    return pl.pallas_call(
        functools.partial(_router_body, n_experts=n_e),
        grid=(nt // tm,),
        in_specs=[pl.BlockSpec((tm, d), lambda i: (i, 0)),
                  pl.BlockSpec((d, LANES), lambda i: (0, 0)),
                  pl.BlockSpec((d, LANES), lambda i: (0, 0))],
        out_specs=pl.BlockSpec((tm, LANES), lambda i: (i, 0)),
        out_shape=jax.ShapeDtypeStruct((nt, LANES), F32),
        compiler_params=_cp(("parallel",), 32),
        name="router",
    )(h, whi, wlo)


V_ROWS = HEAD_DIM + 16
LOG2E = math.log2(math.e)


def _stack_heads(q, n_grp):
    return jnp.concatenate([q[:, g * HEAD_DIM:(g + 1) * HEAD_DIM] for g in range(n_grp)], axis=0)


def _unstack_store(o_ref, o, n_grp, rows):
    for g in range(n_grp):
        o_ref[:, g * HEAD_DIM:(g + 1) * HEAD_DIM] = o[g * rows:(g + 1) * rows, :].astype(o_ref.dtype)


def _scores_t(k, q):
    return lax.dot_general(k, q, (((1,), (1,)), ((), ())), preferred_element_type=F32)


def _transpose_v(v, n_kv):
    nt = v.shape[0]
    vt = v.reshape(nt, n_kv, HEAD_DIM).transpose(1, 2, 0)
    return jnp.concatenate([vt, jnp.ones((n_kv, V_ROWS - HEAD_DIM, nt), v.dtype)], axis=1)


def _win_body(sink_ref, q_ref, km_ref, k0_ref, kp_ref, kc_ref, vm_ref, v0_ref, vp_ref, vc_ref, o_ref, *,
              n_grp, ctx_blocks, n_lat):
    kvh = pl.program_id(0)
    n = pl.program_id(1)
    blk = q_ref.shape[0]
    qs = _stack_heads(q_ref[...], n_grp)
    keys = jnp.concatenate([km_ref[...], k0_ref[...], kp_ref[...], kc_ref[...]], axis=0)
    vt = jnp.concatenate([vm_ref[...], v0_ref[...], vp_ref[...], vc_ref[...]], axis=1)
    st = _scores_t(keys, qs)
    n_win = 3 * blk
    k_off = lax.broadcasted_iota(jnp.int32, st.shape, 0) - blk
    q_off = lax.broadcasted_iota(jnp.int32, st.shape, 1) % blk
    kpos = (n - ctx_blocks) * blk + k_off
    in_band = ((jnp.abs(q_off - k_off) <= WINDOW) & (kpos >= 0) & (kpos < n_lat) & (n >= ctx_blocks))
    st = jnp.where(in_band | (k_off >= n_win - blk), st, NEG)
    sink = jnp.concatenate(
        [jnp.full((1, blk), sink_ref[kvh * n_grp + g] * LOG2E, F32) for g in range(n_grp)], axis=1)
    m = jnp.maximum(jnp.max(st, axis=0, keepdims=True), sink)
    pt = jnp.exp2(st - m).astype(BF16)
    acc = jnp.dot(vt, pt, preferred_element_type=F32)
    den = acc[HEAD_DIM:HEAD_DIM + 1, :] + jnp.exp2(sink - m)
    _unstack_store(o_ref, (acc[0:HEAD_DIM, :] / den).T, n_grp, blk)


def _window_attention(proj, sink, n_ctx, q_col, k_col, v_col, n_kv, n_grp):
    nt = proj.shape[0]
    blk = WINDOW
    nb = nt // blk
    cb = n_ctx // blk
    qc, kc = q_col // (n_grp * HEAD_DIM), k_col // HEAD_DIM
    vt = _transpose_v(proj[:, v_col:v_col + n_kv * HEAD_DIM], n_kv)
    lo = lambda n: jnp.maximum(n - 1, cb)
    hi = lambda n: jnp.minimum(n + 1, nb - 1)
    same = lambda n: n
    k_tile = lambda rowf: pl.BlockSpec((blk, HEAD_DIM), lambda h, n: (rowf(n), kc + h))
    v_tile = lambda rowf: pl.BlockSpec((None, V_ROWS, blk), lambda h, n: (h, 0, rowf(n)))
    body = functools.partial(_win_body, n_grp=n_grp, ctx_blocks=cb, n_lat=nt - n_ctx)
    return pl.pallas_call(
        body,
        grid=(n_kv, nb),
        in_specs=[pl.BlockSpec(memory_space=pltpu.SMEM),
                  pl.BlockSpec((blk, n_grp * HEAD_DIM), lambda h, n: (n, qc + h)),
                  k_tile(lo), k_tile(same), k_tile(hi),
                  pl.BlockSpec((n_ctx, HEAD_DIM), lambda h, n: (0, kc + h)),
                  v_tile(lo), v_tile(same), v_tile(hi),
                  pl.BlockSpec((None, V_ROWS, n_ctx), lambda h, n: (h, 0, 0))],
        out_specs=pl.BlockSpec((blk, n_grp * HEAD_DIM), lambda h, n: (n, h)),
        out_shape=jax.ShapeDtypeStruct((nt, n_kv * n_grp * HEAD_DIM), BF16),
        compiler_params=_cp(("parallel", "parallel"), 32),
        name="window_attn",
    )(sink, proj, proj, proj, proj, proj, vt, vt, vt, vt)


def _flash_body(q_ref, k_ref, vt_ref, o_ref, s_ref, acc_ref, *, n_grp, tk, n_chunks, n_ctx):
    qi = pl.program_id(1)
    tq = q_ref.shape[0]
    qs = _stack_heads(q_ref[...], n_grp)
    is_ctx = qi == 0

    def scores(c):
        start = c * tk if isinstance(c, int) else pl.multiple_of(c * tk, tk)
        return _scores_t(k_ref[pl.ds(start, tk), :], qs)

    def absorb(c, slot, m_old):
        m_new = jnp.maximum(m_old, jnp.max(s_ref[slot], axis=0, keepdims=True))
        alpha = jnp.exp2(m_old - m_new)
        pt = jnp.exp2(s_ref[slot] - m_new).astype(BF16)
        acc_ref[...] = alpha * acc_ref[...] + jnp.dot(vt_ref[c], pt, preferred_element_type=F32)
        return m_new

    acc_ref[...] = jnp.zeros(acc_ref.shape, F32)
    st0 = scores(0)
    row = lax.broadcasted_iota(jnp.int32, st0.shape, 0)
    s_ref[0] = jnp.where(row < jnp.where(is_ctx, n_ctx, tk), st0, NEG)
    m = jnp.full((1, qs.shape[0]), NEG, F32)

    def pair(j, m):
        c = 2 * j
        s_ref[1] = scores(c + 1)
        m = absorb(c, 0, m)
        s_ref[0] = scores(c + 2)
        return absorb(c + 1, 1, m)

    n_pairs = (n_chunks - 1) // 2
    m = lax.fori_loop(0, jnp.where(is_ctx, 0, n_pairs), pair, m)
    if (n_chunks - 1) % 2 == 0:
        absorb(jnp.where(is_ctx, 0, n_chunks - 1), 0, m)
    else:
        s_ref[1] = jnp.where(is_ctx, NEG, scores(n_chunks - 1))
        m = absorb(jnp.where(is_ctx, 0, n_chunks - 2), 0, m)
        absorb(n_chunks - 1, 1, m)
    acc = acc_ref[...]
    _unstack_store(o_ref, (acc[0:HEAD_DIM, :] / acc[HEAD_DIM:HEAD_DIM + 1, :]).T, n_grp, tq)


def _flash_attention(proj, n_ctx, q_col, k_col, v_col, n_kv, n_grp):
    nt = proj.shape[0]
    tq = n_ctx
    tk = _pick(nt, (1280, 256))
    n_chunks = nt // tk
    qc, kc = q_col // (n_grp * HEAD_DIM), k_col // HEAD_DIM
    vt = _transpose_v(proj[:, v_col:v_col + n_kv * HEAD_DIM], n_kv)
    vt = vt.reshape(n_kv, V_ROWS, n_chunks, tk).transpose(0, 2, 1, 3)
    body = functools.partial(_flash_body, n_grp=n_grp, tk=tk, n_chunks=n_chunks, n_ctx=n_ctx)
    return pl.pallas_call(
        body,
        grid=(n_kv, nt // tq),
        in_specs=[pl.BlockSpec((tq, n_grp * HEAD_DIM), lambda h, i: (i, qc + h)),
                  pl.BlockSpec((nt, HEAD_DIM), lambda h, i: (0, kc + h)),
                  pl.BlockSpec((None, n_chunks, V_ROWS, tk), lambda h, i: (h, 0, 0, 0))],
        out_specs=pl.BlockSpec((tq, n_grp * HEAD_DIM), lambda h, i: (i, h)),
        out_shape=jax.ShapeDtypeStruct((nt, n_kv * n_grp * HEAD_DIM), BF16),
        scratch_shapes=[pltpu.VMEM((2, tk, n_grp * tq), F32), pltpu.VMEM((V_ROWS, n_grp * tq), F32)],
        compiler_params=_cp(("parallel", "parallel"), 56),
        name="flash_attn",
    )(proj, proj, vt)


def _s5_weights(lam_re, lam_im, log_dt, b_re, b_im, c_re, c_im, d_skip):
    hp = lax.Precision.HIGHEST
    n_g, n_p = lam_re.shape[1], lam_re.shape[2]
    n_h = b_re.shape[-1]
    ell = S5_CHUNK
    dt = jnp.exp(log_dt)[..., None]
    ar, ai = lam_re * dt, lam_im * dt
    k = jnp.arange(ell + 1, dtype=F32)[:, None, None, None]
    mag = jnp.exp(k * ar[None])
    lp_re, lp_im = mag * jnp.cos(k * ai[None]), mag * jnp.sin(k * ai[None])
    nr, ni = lp_re[1] - 1.0, lp_im[1]
    den = lam_re * lam_re + lam_im * lam_im
    fr, fi = (nr * lam_re + ni * lam_im) / den, (ni * lam_re - nr * lam_im) / den
    bb_re = fr[..., None] * b_re - fi[..., None] * b_im
    bb_im = fr[..., None] * b_im + fi[..., None] * b_re
    lpe_re, lpe_im = lp_re[:, :, :, None, :], lp_im[:, :, :, None, :]
    cl_re = c_re[None] * lpe_re - c_im[None] * lpe_im
    cl_im = c_re[None] * lpe_im + c_im[None] * lpe_re
    m = (jnp.einsum('kdghp,dgpi->kdghi', cl_re, bb_re, precision=hp)
         - jnp.einsum('kdghp,dgpi->kdghi', cl_im, bb_im, precision=hp))
    t_in = jnp.arange(ell)[:, None]
    t_out = jnp.arange(ell)[None, :]
    lag = t_out - t_in
    sel = lambda cond: cond[:, :, None, None, None]
    toep = (jnp.where(sel(lag >= 0), m[:, 0][jnp.clip(lag, 0, ell)], 0.0)
            + jnp.where(sel(lag <= 0), m[:, 1][jnp.clip(-lag, 0, ell)], 0.0))
    gl = LANES // n_h
    n_q = n_g // gl
    kdim = ell * gl * n_h
    eye = jnp.eye(gl, dtype=F32)
    toep = toep.reshape(ell, ell, n_q, gl, n_h, n_h)
    toep_q = jnp.einsum('abqghi,gk->qagibkh', toep, eye).reshape(n_q, kdim, kdim)

    def state_in(direction, powers):
        pr = lp_re[powers, direction].transpose(1, 0, 2)[:, :, None, :]
        pi = lp_im[powers, direction].transpose(1, 0, 2)[:, :, None, :]
        br = bb_re[direction].transpose(0, 2, 1)[:, None, :, :]
        bi = bb_im[direction].transpose(0, 2, 1)[:, None, :, :]
        return pr * br - pi * bi, pr * bi + pi * br

    steps = jnp.arange(ell)
    f_re, f_im = state_in(0, ell - 1 - steps)
    g_re, g_im = state_in(1, steps)
    w_in = jnp.stack([f_re, f_im, g_re, g_im], axis=3)
    w_in = w_in.reshape(n_q, gl, ell, n_h, 4, n_p)
    w_in = jnp.einsum('qgaikp,gj->qagijkp', w_in, eye)
    w_in = w_in.reshape(n_q, ell, gl, n_h, gl // 2, 2, 4, n_p).transpose(0, 1, 2, 3, 4, 6, 5, 7)
    w_in_q = w_in.reshape(n_q, kdim, gl * 4 * n_p)

    def state_out(direction, powers):
        wr = cl_re[powers, direction].transpose(1, 3, 0, 2)
        wi = -cl_im[powers, direction].transpose(1, 3, 0, 2)
        w = jnp.stack([wr, wi], axis=1).reshape(n_q, gl // 2, 2, 2, n_p, ell, n_h)
        w = jnp.einsum('qrsepah,rsj->qrespajh', w, eye.reshape(gl // 2, 2, gl))
        return w.reshape(n_q, gl * 2 * n_p, kdim)

    w_of = state_out(0, steps + 1)
    w_ob = state_out(1, ell - steps)
    lam_l = jnp.stack([lp_re[ell, 0], lp_im[ell, 0], lp_re[ell, 1], lp_im[ell, 1]], axis=0)
    lam_l = lam_l.reshape(4, n_g // 2, 2 * n_p)
    d_q = jnp.broadcast_to(d_skip.reshape(n_q, 1, gl, n_h), (n_q, ell, gl, n_h)).reshape(n_q, 1, kdim)
    return toep_q.astype(BF16), w_in_q.astype(BF16), w_of.astype(BF16), w_ob.astype(BF16), lam_l, d_q


def _s5_chunk_rows(u_ref, n_rows):
    return jnp.concatenate([u_ref[pl.ds(t, n_rows, stride=S5_CHUNK), :] for t in range(S5_CHUNK)], axis=1)


def _s5_in_body(u_ref, w_ref, x_ref):
    uc = _s5_chunk_rows(u_ref, x_ref.shape[0]).astype(BF16)
    x_ref[...] = jnp.dot(uc, w_ref[...], preferred_element_type=F32)


def _s5_scan_body(xf_ref, xb_ref, lam_ref, sf_ref, sb_ref, st_ref):
    @pl.when(pl.program_id(0) == 0)
    def _():
        st_ref[...] = jnp.zeros(st_ref.shape, F32)

    half = xf_ref.shape[2] // 2
    rows = xf_ref.shape[0]

    def run(x_ref, s_ref, base, order):
        ar, ai = lam_ref[base], lam_ref[base + 1]
        sr, si = st_ref[base], st_ref[base + 1]
        for r in order:
            s_ref[r, :, 0:half] = sr
            s_ref[r, :, half:2 * half] = si
            xr, xi = x_ref[r, :, 0:half], x_ref[r, :, half:2 * half]
            sr, si = ar * sr - ai * si + xr, ar * si + ai * sr + xi
        st_ref[base] = sr
        st_ref[base + 1] = si

    run(xf_ref, sf_ref, 0, range(rows))
    run(xb_ref, sb_ref, 2, range(rows - 1, -1, -1))


def _s5_out_body(u_ref, sf_ref, sb_ref, toep_ref, wof_ref, wob_ref, d_ref, y_ref):
    n_rows = sf_ref.shape[0]
    uc = _s5_chunk_rows(u_ref, n_rows)
    y = jnp.dot(uc.astype(BF16), toep_ref[...], preferred_element_type=F32)
    y += jnp.dot(sf_ref[...].astype(BF16), wof_ref[...], preferred_element_type=F32)
    y += jnp.dot(sb_ref[...].astype(BF16), wob_ref[...], preferred_element_type=F32)
    y += uc * d_ref[...]
    for t in range(S5_CHUNK):
        y_ref[pl.ds(t, n_rows, stride=S5_CHUNK), :] = y[:, t * LANES:(t + 1) * LANES]


def _s5(u, n_ctx, weights):
    toep_q, w_in_q, w_of, w_ob, lam_l, d_q = weights
    nt, width = u.shape
    ell = S5_CHUNK
    n_q, kdim, sdim_q = w_in_q.shape
    n_pair = lam_l.shape[1]
    sdim = n_q * sdim_q // n_pair
    n_chunk = nt // ell
    cb = _pick(n_chunk, (208, 240, 80))
    u_spec = pl.BlockSpec((cb * ell, LANES), lambda q, i: (i, q))
    per_q = lambda r, c: pl.BlockSpec((None, r, c), lambda q, i: (q, 0, 0))
    x = pl.pallas_call(
        _s5_in_body,
        grid=(n_q, n_chunk // cb),
        in_specs=[u_spec, per_q(kdim, sdim_q)],
        out_specs=pl.BlockSpec((cb, sdim_q), lambda q, i: (i, q)),
        out_shape=jax.ShapeDtypeStruct((n_chunk, n_q * sdim_q), F32),
        compiler_params=_cp(("parallel", "parallel"), 40),
        name="s5_state_in",
    )(u, w_in_q)
    xt = x.reshape(n_chunk, n_pair, sdim)
    rb = n_ctx // ell
    nb = n_chunk // rb
    half = sdim // 2
    bwd = lambda b: (jnp.where(b == 0, 0, nb - b), 0, 0)
    sf, sb = pl.pallas_call(
        _s5_scan_body,
        grid=(nb,),
        in_specs=[pl.BlockSpec((rb, n_pair, half), lambda b: (b, 0, 0)),
                  pl.BlockSpec((rb, n_pair, half), lambda b: (jnp.where(b == 0, 0, nb - b), 0, 1)),
                  pl.BlockSpec((4, n_pair, half // 2), lambda b: (0, 0, 0))],
        out_specs=[pl.BlockSpec((rb, n_pair, half), lambda b: (b, 0, 0)),
                   pl.BlockSpec((rb, n_pair, half), bwd)],
        out_shape=[jax.ShapeDtypeStruct((n_chunk, n_pair, half), F32),
                   jax.ShapeDtypeStruct((n_chunk, n_pair, half), F32)],
        scratch_shapes=[pltpu.VMEM((4, n_pair, half // 2), F32)],
        compiler_params=_cp(("arbitrary",), 32),
        name="s5_scan",
    )(xt, xt, lam_l)
    sdim_h = sdim_q // 2
    s_spec = pl.BlockSpec((cb, sdim_h), lambda q, i: (i, q))
    return pl.pallas_call(
        _s5_out_body,
        grid=(n_q, n_chunk // cb),
        in_specs=[u_spec, s_spec, s_spec,
                  per_q(kdim, kdim), per_q(sdim_h, kdim), per_q(sdim_h, kdim), per_q(1, kdim)],
        out_specs=u_spec,
        out_shape=jax.ShapeDtypeStruct((nt, width), F32),
        compiler_params=_cp(("parallel", "parallel"), 56),
        name="s5_out",
    )(u, sf.reshape(n_chunk, n_pair * half), sb.reshape(n_chunk, n_pair * half), toep_q, w_of, w_ob, d_q)


def _glu_body(y_ref, w_ref, b_ref, o_ref):
    y = y_ref[...]
    gy = 0.5 * y * (1.0 + jnp.tanh(math.sqrt(2.0 / math.pi) * (y + 0.044715 * (y * y * y))))
    z = jnp.dot(gy.astype(BF16), w_ref[...], preferred_element_type=F32) + b_ref[...]
    o_ref[...] = (gy * jax.nn.sigmoid(z)).astype(BF16)


def _s5_glu(y, glu_w, glu_b):
    nt, width = y.shape
    tm = _pick(nt, (1280, 640, 256))
    return pl.pallas_call(
        _glu_body,
        grid=(nt // tm,),
        in_specs=[pl.BlockSpec((tm, width), lambda i: (i, 0)),
                  pl.BlockSpec((width, width), lambda i: (0, 0)),
                  pl.BlockSpec((1, width), lambda i: (0, 0))],
        out_specs=pl.BlockSpec((tm, width), lambda i: (i, 0)),
        out_shape=jax.ShapeDtypeStruct((nt, width), BF16),
        compiler_params=_cp(("parallel",), 40),
        name="s5_glu",
    )(y, glu_w, glu_b.reshape(1, width))


def _rope_tables(seq, n_ctx):
    t = jnp.arange(seq)
    row_id = (t // GRID_W).astype(F32)
    col_id = (t % GRID_W).astype(F32)
    n_freq = HEAD_DIM // 4
    inv = ROPE_THETA ** (-jnp.arange(n_freq, dtype=F32) / n_freq)
    ang_r = row_id[:, None] * inv[None, :]
    ang_c = col_id[:, None] * inv[None, :]
    cos_t = jnp.concatenate([jnp.cos(ang_r), jnp.cos(ang_r), jnp.cos(ang_c), jnp.cos(ang_c)], axis=1)
    sin_t = jnp.concatenate([-jnp.sin(ang_r), jnp.sin(ang_r), -jnp.sin(ang_c), jnp.sin(ang_c)], axis=1)
    cos_t = jnp.concatenate([jnp.ones((n_ctx, HEAD_DIM), F32), cos_t], axis=0)
    sin_t = jnp.concatenate([jnp.zeros((n_ctx, HEAD_DIM), F32), sin_t], axis=0)
    return cos_t, sin_t


def kernel(x, c, ctx, c_ctx, w_mod, b_mod, norm_g, ab_w_in, ab_w_out, s5_lam_re, s5_lam_im, s5_log_dt, s5_b_re, s5_b_im, s5_c_re, s5_c_im, s5_d, glu_w, glu_b, attn_sink, ffn_w1, ffn_w3, ffn_w2, c_w_in, c_w_out, q_norm, k_norm, router_w, moe_w1, moe_w3, moe_w2):
    assert x.shape[0] == 1 and ctx.shape[0] == 1
    seq, d = x.shape[1], x.shape[2]
    n_ctx = ctx.shape[1]
    assert n_ctx == ROW_TILE
    depth = w_mod.shape[0]
    s5_width = s5_d.shape[1] * s5_d.shape[2]
    b_heads = attn_sink.shape[1]
    b_kv = (ab_w_in.shape[2] - s5_width - b_heads * HEAD_DIM) // (2 * HEAD_DIM)
    c_heads = c_w_out.shape[1] // HEAD_DIM
    c_kv = (c_w_in.shape[2] - c_heads * HEAD_DIM) // (2 * HEAD_DIM)
    q_scale = HEAD_DIM ** -0.5 * LOG2E

    cond8 = jnp.zeros((8, d), F32).at[0].set(c_ctx).at[1].set(c[0])
    mod4 = _adaln(cond8, w_mod, b_mod).reshape(depth, 8, N_MOD, d)
    cos_t, sin_t = _rope_tables(seq, n_ctx)
    xa, h = _prenorm(x[0], ctx[0], norm_g, mod4)

    def col_vec(pieces):
        return jnp.concatenate([jnp.broadcast_to(jnp.asarray(v, F32), (n,)) for v, n in pieces]).reshape(1, -1)

    for i in range(depth):
        j = i // 2
        if i % 2 == 0:
            q0 = s5_width
            k0 = q0 + b_heads * HEAD_DIM
            v0 = k0 + b_kv * HEAD_DIM
            n_in = v0 + b_kv * HEAD_DIM
            proj = _inproj(h, ab_w_in[j].astype(BF16), cos_t, sin_t,
                           jnp.ones((1, n_in), F32),
                           col_vec([(1.0, q0), (q_scale, k0 - q0), (1.0, n_in - k0)]),
                           q0, v0, False)
            s5w = _s5_weights(s5_lam_re[j], s5_lam_im[j], s5_log_dt[j], s5_b_re[j], s5_b_im[j],
                              s5_c_re[j], s5_c_im[j], s5_d[j])
            a = _s5_glu(_s5(proj[:, :s5_width].astype(F32), n_ctx, s5w), glu_w[j].astype(BF16), glu_b[j])
            o = _window_attention(proj, attn_sink[j], n_ctx, q0, k0, v0, b_kv, b_heads // b_kv)
            w_out = ab_w_out[j].astype(BF16)
            y = _matmul([a, o], [w_out[:s5_width], w_out[s5_width:]], (1280, 640, 256), 512, F32, 48, "ab_out")
        else:
            k0 = c_heads * HEAD_DIM
            v0 = k0 + c_kv * HEAD_DIM
            n_in = v0 + c_kv * HEAD_DIM
            proj = _inproj(h, c_w_in[j].astype(BF16), cos_t, sin_t,
                           col_vec([(jnp.tile(q_norm[j], c_heads), k0), (jnp.tile(k_norm[j], c_kv), v0 - k0),
                                    (1.0, n_in - v0)]),
                           col_vec([(q_scale, k0), (1.0, n_in - k0)]),
                           0, v0, True)
            o = _flash_attention(proj, n_ctx, 0, k0, v0, c_kv, c_heads // c_kv)
            y = _matmul([o], [c_w_out[j].astype(BF16)], (1280, 640, 256), 512, F32, 48, "c_out")
        xa, h = _resid(xa, y, norm_g, mod4, i, 1, 2, i, 2, 3)
        if i % 2 == 0:
            g = _swiglu_up(h, ffn_w1[j].astype(BF16)[None], ffn_w3[j].astype(BF16)[None])
            f = _matmul([g], [ffn_w2[j].astype(BF16)], (1280, 640, 256), 512, F32, 56, "ffn_down")
        else:
            gates = _router(h, router_w[j])
            g = _swiglu_up(h, moe_w1[j].astype(BF16), moe_w3[j].astype(BF16), gates)
            w2 = moe_w2[j].astype(BF16).reshape(-1, d)
            f = _matmul([g], [w2], (640, 256), 512, F32, 56, "moe_down")
        if i + 1 < depth:
            xa, h = _resid(xa, f, norm_g, mod4, i, 3, 5, i + 1, 0, 0)
        else:
            out = _final(xa, f, norm_g, mod4, i, 3, 5, n_ctx)
    return out[None]
```

```python
import functools
import math

import jax
import jax.numpy as jnp
from jax import lax
from jax.experimental import pallas as pl
from jax.experimental.pallas import tpu as pltpu

F32 = jnp.float32
BF16 = jnp.bfloat16

HEAD_DIM = 128
GRID_W = 64
ROPE_THETA = 10000.0
NORM_EPS = 1e-6
NEG = -1e30
N_MOD = 6
S5_GROUP = 16
S5_STATE = 64
S5_CHUNK = 16
WINDOW = 128
TOP_K = 2
LANES = 128
ROW_TILE = 256
MIB = 1024 * 1024


def _cp(sem, vmem_mib):
    return pltpu.CompilerParams(dimension_semantics=sem, vmem_limit_bytes=vmem_mib * MIB)


def _pick(n, cands):
    for t in cands:
        if n % t == 0:
            return t
    raise ValueError(f"no tile for {n} in {cands}")


def _rms(v):
    return v * lax.rsqrt(jnp.mean(v * v, axis=-1, keepdims=True) + NORM_EPS)


def _adaln_body(cond_ref, w_ref, b_ref, o_ref):
    c = cond_ref[...]
    a = (c * jax.nn.sigmoid(c)).astype(BF16)
    o_ref[...] = jnp.dot(a, w_ref[...].astype(BF16), preferred_element_type=F32) + b_ref[...]


def _adaln(cond8, w_mod, b_mod):
    depth, d, n6 = w_mod.shape
    tn = 512
    return pl.pallas_call(
        _adaln_body,
        grid=(depth, n6 // tn),
        in_specs=[pl.BlockSpec((8, d), lambda l, j: (0, 0)),
                  pl.BlockSpec((None, d, tn), lambda l, j: (l, 0, j)),
                  pl.BlockSpec((None, 1, tn), lambda l, j: (l, 0, j))],
        out_specs=pl.BlockSpec((None, 8, tn), lambda l, j: (l, 0, j)),
        out_shape=jax.ShapeDtypeStruct((depth, 8, n6), F32),
        compiler_params=_cp(("parallel", "parallel"), 40),
        name="adaln",
    )(cond8, w_mod, b_mod.reshape(depth, 1, n6))


def _prenorm_body(x_ref, ctx_ref, g_ref, mod_ref, xa_ref, h_ref):
    i = pl.program_id(0)

    def emit(v):
        xa_ref[...] = v
        n = _rms(v) * g_ref[0:1, :]
        h_ref[...] = (n * (1.0 + mod_ref[1:2, :]) + mod_ref[0:1, :]).astype(BF16)

    @pl.when(i == 0)
    def _():
        emit(ctx_ref[...])

    @pl.when(i > 0)
    def _():
        emit(x_ref[...])


def _prenorm(x2, ctx2, norm_g, mod4):
    seq, d = x2.shape
    nc = ctx2.shape[0]
    nt = seq + nc
    tr = ROW_TILE
    return pl.pallas_call(
        _prenorm_body,
        grid=(nt // tr,),
        in_specs=[pl.BlockSpec((tr, d), lambda i: (jnp.maximum(i - 1, 0), 0)),
                  pl.BlockSpec((tr, d), lambda i: (0, 0)),
                  pl.BlockSpec((None, 4, d), lambda i: (0, 0, 0)),
                  pl.BlockSpec((None, None, N_MOD, d), lambda i: (0, jnp.minimum(i, 1), 0, 0))],
        out_specs=[pl.BlockSpec((tr, d), lambda i: (i, 0)),
                   pl.BlockSpec((tr, d), lambda i: (i, 0))],
        out_shape=[jax.ShapeDtypeStruct((nt, d), F32), jax.ShapeDtypeStruct((nt, d), BF16)],
        compiler_params=_cp(("parallel",), 40),
        name="prenorm",
    )(x2, ctx2, norm_g, mod4)


def _resid_body(xa_ref, y_ref, gcur_ref, mcur_ref, gnext_ref, mnext_ref, xo_ref, h_ref, *,
                post_idx, gate_idx, next_g_idx, next_shift_idx):
    yn = _rms(y_ref[...].astype(F32)) * gcur_ref[post_idx:post_idx + 1, :]
    x = xa_ref[...] + mcur_ref[gate_idx:gate_idx + 1, :] * yn
    xo_ref[...] = x
    n = _rms(x) * gnext_ref[next_g_idx:next_g_idx + 1, :]
    h_ref[...] = (n * (1.0 + mnext_ref[next_shift_idx + 1:next_shift_idx + 2, :])
                  + mnext_ref[next_shift_idx:next_shift_idx + 1, :]).astype(BF16)


def _resid(xa, y, norm_g, mod4, layer, post_idx, gate_idx, next_layer, next_g_idx, next_shift_idx):
    nt, d = xa.shape
    tr = ROW_TILE
    row = lambda i: (i, 0)
    body = functools.partial(_resid_body, post_idx=post_idx, gate_idx=gate_idx,
                             next_g_idx=next_g_idx, next_shift_idx=next_shift_idx)
    return pl.pallas_call(
        body,
        grid=(nt // tr,),
        in_specs=[pl.BlockSpec((tr, d), row),
                  pl.BlockSpec((tr, d), row),
                  pl.BlockSpec((None, 4, d), lambda i: (layer, 0, 0)),
                  pl.BlockSpec((None, None, N_MOD, d), lambda i: (layer, jnp.minimum(i, 1), 0, 0)),
                  pl.BlockSpec((None, 4, d), lambda i: (next_layer, 0, 0)),
                  pl.BlockSpec((None, None, N_MOD, d), lambda i: (next_layer, jnp.minimum(i, 1), 0, 0))],
        out_specs=[pl.BlockSpec((tr, d), row), pl.BlockSpec((tr, d), row)],
        out_shape=[jax.ShapeDtypeStruct((nt, d), F32), jax.ShapeDtypeStruct((nt, d), BF16)],
        input_output_aliases={0: 0},
        compiler_params=_cp(("parallel",), 48),
        name="resid",
    )(xa, y, norm_g, mod4, norm_g, mod4)


def _final_body(xa_ref, y_ref, g_ref, m_ref, o_ref, *, post_idx, gate_idx):
    yn = _rms(y_ref[...].astype(F32)) * g_ref[post_idx:post_idx + 1, :]
    o_ref[...] = xa_ref[...] + m_ref[gate_idx:gate_idx + 1, :] * yn


def _final(xa, y, norm_g, mod4, layer, post_idx, gate_idx, n_ctx):
    nt, d = xa.shape
    tr = ROW_TILE
    off = n_ctx // tr
    body = functools.partial(_final_body, post_idx=post_idx, gate_idx=gate_idx)
    return pl.pallas_call(
        body,
        grid=((nt - n_ctx) // tr,),
        in_specs=[pl.BlockSpec((tr, d), lambda i: (i + off, 0)),
                  pl.BlockSpec((tr, d), lambda i: (i + off, 0)),
                  pl.BlockSpec((None, 4, d), lambda i: (layer, 0, 0)),
                  pl.BlockSpec((None, None, N_MOD, d), lambda i: (layer, 1, 0, 0))],
        out_specs=pl.BlockSpec((tr, d), lambda i: (i, 0)),
        out_shape=jax.ShapeDtypeStruct((nt - n_ctx, d), F32),
        compiler_params=_cp(("parallel",), 40),
        name="final",
    )(xa, y, norm_g, mod4)


def _inproj_body(h_ref, w_ref, cos_ref, sin_ref, gain_ref, scale_ref, o_ref, *, rope_lo, rope_hi, use_norm):
    j = pl.program_id(1)
    acc = jnp.dot(h_ref[...], w_ref[...], preferred_element_type=F32)
    is_rope = jnp.logical_and(j >= rope_lo, j < rope_hi)

    @pl.when(jnp.logical_not(is_rope))
    def _():
        o_ref[...] = acc.astype(BF16)

    @pl.when(is_rope)
    def _():
        c = cos_ref[...]
        s = sin_ref[...]
        lane = lax.broadcasted_iota(jnp.int32, c.shape, 1)
        first = (lane & 32) == 0
        for hh in range(acc.shape[1] // HEAD_DIM):
            sl = slice(hh * HEAD_DIM, (hh + 1) * HEAD_DIM)
            blk = acc[:, sl]
            if use_norm:
                blk = _rms(blk) * gain_ref[:, sl]
            rot = jnp.where(first, pltpu.roll(blk, HEAD_DIM - 32, 1), pltpu.roll(blk, 32, 1))
            o_ref[:, sl] = ((blk * c + rot * s) * scale_ref[:, sl]).astype(BF16)


def _inproj(h, w, cos_t, sin_t, gain, scale, rope_lo, rope_hi, use_norm):
    nt, d = h.shape
    n = w.shape[1]
    tm = _pick(nt, (1280, 640, 256))
    tn = 512
    body = functools.partial(_inproj_body, rope_lo=rope_lo // tn, rope_hi=rope_hi // tn, use_norm=use_norm)
    return pl.pallas_call(
        body,
        grid=(nt // tm, n // tn),
        in_specs=[pl.BlockSpec((tm, d), lambda i, j: (i, 0)),
                  pl.BlockSpec((d, tn), lambda i, j: (0, j)),
                  pl.BlockSpec((tm, HEAD_DIM), lambda i, j: (i, 0)),
                  pl.BlockSpec((tm, HEAD_DIM), lambda i, j: (i, 0)),
                  pl.BlockSpec((1, tn), lambda i, j: (0, j)),
                  pl.BlockSpec((1, tn), lambda i, j: (0, j))],
        out_specs=pl.BlockSpec((tm, tn), lambda i, j: (i, j)),
        out_shape=jax.ShapeDtypeStruct((nt, n), BF16),
        compiler_params=_cp(("parallel", "arbitrary"), 48),
        name="inproj",
    )(h, w, cos_t, sin_t, gain, scale)


def _mm_body(*refs, n_a):
    o_ref = refs[2 * n_a]
    acc = jnp.dot(refs[0][...], refs[n_a][...], preferred_element_type=F32)
    for t in range(1, n_a):
        acc += jnp.dot(refs[t][...], refs[n_a + t][...], preferred_element_type=F32)
    o_ref[...] = acc.astype(o_ref.dtype)


def _matmul(a_list, w_list, tm_cands, tn, out_dtype, vmem_mib, name):
    nt = a_list[0].shape[0]
    n = w_list[0].shape[1]
    tm = _pick(nt, tm_cands)
    in_specs = [pl.BlockSpec((tm, a.shape[1]), lambda i, j: (i, 0)) for a in a_list]
    in_specs += [pl.BlockSpec((w.shape[0], tn), lambda i, j: (0, j)) for w in w_list]
    return pl.pallas_call(
        functools.partial(_mm_body, n_a=len(a_list)),
        grid=(nt // tm, n // tn),
        in_specs=in_specs,
        out_specs=pl.BlockSpec((tm, tn), lambda i, j: (i, j)),
        out_shape=jax.ShapeDtypeStruct((nt, n), out_dtype),
        compiler_params=_cp(("parallel", "arbitrary"), vmem_mib),
        name=name,
    )(*a_list, *w_list)


def _swiglu_up_body(*refs, gated):
    if gated:
        h_ref, w1_ref, w3_ref, gate_ref, o_ref = refs
    else:
        h_ref, w1_ref, w3_ref, o_ref = refs
    h = h_ref[...]
    a = jnp.dot(h, w1_ref[...], preferred_element_type=F32)
    b = jnp.dot(h, w3_ref[...], preferred_element_type=F32)
    r = a * jax.nn.sigmoid(a) * b
    if gated:
        g = gate_ref[...]
        lane = lax.broadcasted_iota(jnp.int32, g.shape, 1)
        r = r * jnp.sum(jnp.where(lane == pl.program_id(1), g, 0.0), axis=-1, keepdims=True)
    o_ref[...] = r.astype(BF16)


def _swiglu_up(h, w1, w3, gates=None):
    nt, d = h.shape
    n_e, _, f = w1.shape
    gated = gates is not None
    if gated:
        tm, tn = _pick(nt, (640, 256)), f
        w_spec = pl.BlockSpec((None, d, tn), lambda i, j: (j, 0, 0))
    else:
        tm, tn = _pick(nt, (1280, 640, 256)), 512
        w_spec = pl.BlockSpec((None, d, tn), lambda i, j: (0, 0, j))
    in_specs = [pl.BlockSpec((tm, d), lambda i, j: (i, 0)), w_spec, w_spec]
    args = [h, w1, w3]
    if gated:
        in_specs.append(pl.BlockSpec((tm, LANES), lambda i, j: (i, 0)))
        args.append(gates)
    return pl.pallas_call(
        functools.partial(_swiglu_up_body, gated=gated),
        grid=(nt // tm, n_e * f // tn),
        in_specs=in_specs,
        out_specs=pl.BlockSpec((tm, tn), lambda i, j: (i, j)),
        out_shape=jax.ShapeDtypeStruct((nt, n_e * f), BF16),
        compiler_params=_cp(("parallel", "arbitrary"), 52),
        name="swiglu_up",
    )(*args)


def _router_body(h_ref, whi_ref, wlo_ref, g_ref, *, n_experts):
    h = h_ref[...]
    logits = (jnp.dot(h, whi_ref[...], preferred_element_type=F32)
              + jnp.dot(h, wlo_ref[...], preferred_element_type=F32))
    lane = lax.broadcasted_iota(jnp.int32, logits.shape, 1)
    lg = jnp.where(lane < n_experts, logits, NEG)
    v1 = jnp.max(lg, axis=-1, keepdims=True)
    i1 = jnp.min(jnp.where(lg == v1, lane, LANES), axis=-1, keepdims=True)
    lg2 = jnp.where(lane == i1, NEG, lg)
    v2 = jnp.max(lg2, axis=-1, keepdims=True)
    i2 = jnp.min(jnp.where(lg2 == v2, lane, LANES), axis=-1, keepdims=True)
    e = jnp.exp(v2 - v1)
    den = 1.0 + e
    g_ref[...] = jnp.where(lane == i1, 1.0 / den, 0.0) + jnp.where(lane == i2, e / den, 0.0)


def _router(h, router_w):
    nt, d = h.shape
    n_e = router_w.shape[1]
    whi = router_w.astype(BF16)
    wlo = (router_w - whi.astype(F32)).astype(BF16)
    pad = ((0, 0), (0, LANES - n_e))
    whi = jnp.pad(whi, pad)
    wlo = jnp.pad(wlo, pad)
    tm = _pick(nt, (640, 256))
    return pl.pallas_call(
        functools.partial(_router_body, n_experts=n_e),
        grid=(nt // tm,),
        in_specs=[pl.BlockSpec((tm, d), lambda i: (i, 0)),
                  pl.BlockSpec((d, LANES), lambda i: (0, 0)),
                  pl.BlockSpec((d, LANES), lambda i: (0, 0))],
        out_specs=pl.BlockSpec((tm, LANES), lambda i: (i, 0)),
        out_shape=jax.ShapeDtypeStruct((nt, LANES), F32),
        compiler_params=_cp(("parallel",), 32),
        name="router",
    )(h, whi, wlo)


V_ROWS = HEAD_DIM + 16
LOG2E = math.log2(math.e)


def _stack_heads(q, n_grp):
    return jnp.concatenate([q[:, g * HEAD_DIM:(g + 1) * HEAD_DIM] for g in range(n_grp)], axis=0)


def _unstack_store(o_ref, o, n_grp, rows):
    for g in range(n_grp):
        o_ref[:, g * HEAD_DIM:(g + 1) * HEAD_DIM] = o[g * rows:(g + 1) * rows, :].astype(o_ref.dtype)


def _scores_t(k, q):
    return lax.dot_general(k, q, (((1,), (1,)), ((), ())), preferred_element_type=F32)


def _transpose_v(v, n_kv):
    nt = v.shape[0]
    vt = v.reshape(nt, n_kv, HEAD_DIM).transpose(1, 2, 0)
    return jnp.concatenate([vt, jnp.ones((n_kv, V_ROWS - HEAD_DIM, nt), v.dtype)], axis=1)


def _win_body(sink_ref, q_ref, km_ref, k0_ref, kp_ref, kc_ref, vm_ref, v0_ref, vp_ref, vc_ref, o_ref, *,
              n_grp, ctx_blocks, n_lat):
    kvh = pl.program_id(0)
    n = pl.program_id(1)
    blk = q_ref.shape[0]
    qs = _stack_heads(q_ref[...], n_grp)
    keys = jnp.concatenate([km_ref[...], k0_ref[...], kp_ref[...], kc_ref[...]], axis=0)
    vt = jnp.concatenate([vm_ref[...], v0_ref[...], vp_ref[...], vc_ref[...]], axis=1)
    st = _scores_t(keys, qs)
    n_win = 3 * blk
    k_off = lax.broadcasted_iota(jnp.int32, st.shape, 0) - blk
    q_off = lax.broadcasted_iota(jnp.int32, st.shape, 1) % blk
    kpos = (n - ctx_blocks) * blk + k_off
    in_band = ((jnp.abs(q_off - k_off) <= WINDOW) & (kpos >= 0) & (kpos < n_lat) & (n >= ctx_blocks))
    st = jnp.where(in_band | (k_off >= n_win - blk), st, NEG)
    sink = jnp.concatenate(
        [jnp.full((1, blk), sink_ref[kvh * n_grp + g] * LOG2E, F32) for g in range(n_grp)], axis=1)
    m = jnp.maximum(jnp.max(st, axis=0, keepdims=True), sink)
    pt = jnp.exp2(st - m).astype(BF16)
    acc = jnp.dot(vt, pt, preferred_element_type=F32)
    den = acc[HEAD_DIM:HEAD_DIM + 1, :] + jnp.exp2(sink - m)
    _unstack_store(o_ref, (acc[0:HEAD_DIM, :] / den).T, n_grp, blk)


def _window_attention(proj, sink, n_ctx, q_col, k_col, v_col, n_kv, n_grp):
    nt = proj.shape[0]
    blk = WINDOW
    nb = nt // blk
    cb = n_ctx // blk
    qc, kc = q_col // (n_grp * HEAD_DIM), k_col // HEAD_DIM
    vt = _transpose_v(proj[:, v_col:v_col + n_kv * HEAD_DIM], n_kv)
    lo = lambda n: jnp.maximum(n - 1, cb)
    hi = lambda n: jnp.minimum(n + 1, nb - 1)
    same = lambda n: n
    k_tile = lambda rowf: pl.BlockSpec((blk, HEAD_DIM), lambda h, n: (rowf(n), kc + h))
    v_tile = lambda rowf: pl.BlockSpec((None, V_ROWS, blk), lambda h, n: (h, 0, rowf(n)))
    body = functools.partial(_win_body, n_grp=n_grp, ctx_blocks=cb, n_lat=nt - n_ctx)
    return pl.pallas_call(
        body,
        grid=(n_kv, nb),
        in_specs=[pl.BlockSpec(memory_space=pltpu.SMEM),
                  pl.BlockSpec((blk, n_grp * HEAD_DIM), lambda h, n: (n, qc + h)),
                  k_tile(lo), k_tile(same), k_tile(hi),
                  pl.BlockSpec((n_ctx, HEAD_DIM), lambda h, n: (0, kc + h)),
                  v_tile(lo), v_tile(same), v_tile(hi),
                  pl.BlockSpec((None, V_ROWS, n_ctx), lambda h, n: (h, 0, 0))],
        out_specs=pl.BlockSpec((blk, n_grp * HEAD_DIM), lambda h, n: (n, h)),
        out_shape=jax.ShapeDtypeStruct((nt, n_kv * n_grp * HEAD_DIM), BF16),
        compiler_params=_cp(("parallel", "parallel"), 32),
        name="window_attn",
    )(sink, proj, proj, proj, proj, proj, vt, vt, vt, vt)


def _flash_body(q_ref, k_ref, vt_ref, o_ref, s_ref, acc_ref, *, n_grp, tk, n_chunks, n_ctx):
    qi = pl.program_id(1)
    tq = q_ref.shape[0]
    qs = _stack_heads(q_ref[...], n_grp)
    is_ctx = qi == 0

    def scores(c):
        start = c * tk if isinstance(c, int) else pl.multiple_of(c * tk, tk)
        return _scores_t(k_ref[pl.ds(start, tk), :], qs)

    def absorb(c, slot, m_old):
        m_new = jnp.maximum(m_old, jnp.max(s_ref[slot], axis=0, keepdims=True))
        alpha = jnp.exp2(m_old - m_new)
        pt = jnp.exp2(s_ref[slot] - m_new).astype(BF16)
        acc_ref[...] = alpha * acc_ref[...] + jnp.dot(vt_ref[c], pt, preferred_element_type=F32)
        return m_new

    acc_ref[...] = jnp.zeros(acc_ref.shape, F32)
    st0 = scores(0)
    row = lax.broadcasted_iota(jnp.int32, st0.shape, 0)
    s_ref[0] = jnp.where(row < jnp.where(is_ctx, n_ctx, tk), st0, NEG)
    m = jnp.full((1, qs.shape[0]), NEG, F32)

    def pair(j, m):
        c = 2 * j
        s_ref[1] = scores(c + 1)
        m = absorb(c, 0, m)
        s_ref[0] = scores(c + 2)
        return absorb(c + 1, 1, m)

    n_pairs = (n_chunks - 1) // 2
    m = lax.fori_loop(0, jnp.where(is_ctx, 0, n_pairs), pair, m)
    if (n_chunks - 1) % 2 == 0:
        absorb(jnp.where(is_ctx, 0, n_chunks - 1), 0, m)
    else:
        s_ref[1] = jnp.where(is_ctx, NEG, scores(n_chunks - 1))
        m = absorb(jnp.where(is_ctx, 0, n_chunks - 2), 0, m)
        absorb(n_chunks - 1, 1, m)
    acc = acc_ref[...]
    _unstack_store(o_ref, (acc[0:HEAD_DIM, :] / acc[HEAD_DIM:HEAD_DIM + 1, :]).T, n_grp, tq)


def _flash_attention(proj, n_ctx, q_col, k_col, v_col, n_kv, n_grp):
    nt = proj.shape[0]
    tq = n_ctx
    tk = _pick(nt, (1280, 256))
    n_chunks = nt // tk
    qc, kc = q_col // (n_grp * HEAD_DIM), k_col // HEAD_DIM
    vt = _transpose_v(proj[:, v_col:v_col + n_kv * HEAD_DIM], n_kv)
    vt = vt.reshape(n_kv, V_ROWS, n_chunks, tk).transpose(0, 2, 1, 3)
    body = functools.partial(_flash_body, n_grp=n_grp, tk=tk, n_chunks=n_chunks, n_ctx=n_ctx)
    return pl.pallas_call(
        body,
        grid=(n_kv, nt // tq),
        in_specs=[pl.BlockSpec((tq, n_grp * HEAD_DIM), lambda h, i: (i, qc + h)),
                  pl.BlockSpec((nt, HEAD_DIM), lambda h, i: (0, kc + h)),
                  pl.BlockSpec((None, n_chunks, V_ROWS, tk), lambda h, i: (h, 0, 0, 0))],
        out_specs=pl.BlockSpec((tq, n_grp * HEAD_DIM), lambda h, i: (i, h)),
        out_shape=jax.ShapeDtypeStruct((nt, n_kv * n_grp * HEAD_DIM), BF16),
        scratch_shapes=[pltpu.VMEM((2, tk, n_grp * tq), F32), pltpu.VMEM((V_ROWS, n_grp * tq), F32)],
        compiler_params=_cp(("parallel", "parallel"), 56),
        name="flash_attn",
    )(proj, proj, vt)


def _s5_weights(lam_re, lam_im, log_dt, b_re, b_im, c_re, c_im, d_skip):
    hp = lax.Precision.HIGHEST
    n_g, n_p = lam_re.shape[1], lam_re.shape[2]
    n_h = b_re.shape[-1]
    ell = S5_CHUNK
    dt = jnp.exp(log_dt)[..., None]
    ar, ai = lam_re * dt, lam_im * dt
    k = jnp.arange(ell + 1, dtype=F32)[:, None, None, None]
    mag = jnp.exp(k * ar[None])
    lp_re, lp_im = mag * jnp.cos(k * ai[None]), mag * jnp.sin(k * ai[None])
    nr, ni = lp_re[1] - 1.0, lp_im[1]
    den = lam_re * lam_re + lam_im * lam_im
    fr, fi = (nr * lam_re + ni * lam_im) / den, (ni * lam_re - nr * lam_im) / den
    bb_re = fr[..., None] * b_re - fi[..., None] * b_im
    bb_im = fr[..., None] * b_im + fi[..., None] * b_re
    lpe_re, lpe_im = lp_re[:, :, :, None, :], lp_im[:, :, :, None, :]
    cl_re = c_re[None] * lpe_re - c_im[None] * lpe_im
    cl_im = c_re[None] * lpe_im + c_im[None] * lpe_re
    m = (jnp.einsum('kdghp,dgpi->kdghi', cl_re, bb_re, precision=hp)
         - jnp.einsum('kdghp,dgpi->kdghi', cl_im, bb_im, precision=hp))
    gl = LANES // n_h
    n_q = n_g // gl
    lane = jnp.arange(LANES)
    put_h = (lane[None, None, :] == (jnp.arange(gl)[:, None, None] * n_h
                                     + jnp.arange(n_h)[None, :, None])).astype(F32)
    put_p = (lane[None, None, :] == (jnp.arange(2)[:, None, None] * n_p
                                     + jnp.arange(n_p)[None, :, None])).astype(F32)
    m_lag = jnp.concatenate([m[ell - 1:0:-1, 1], (m[0, 0] + m[0, 1])[None], m[1:ell, 0]], axis=0)
    m_lag = m_lag.reshape(2 * ell - 1, n_q, gl, n_h, n_h)
    toep_c = jnp.einsum('dqghi,gir,ghl->dqrl', m_lag, put_h, put_h, precision=hp)

    def state_in(direction, powers):
        pr = lp_re[powers, direction].transpose(1, 0, 2)[:, :, None, :]
        pi = lp_im[powers, direction].transpose(1, 0, 2)[:, :, None, :]
        br = bb_re[direction].transpose(0, 2, 1)[:, None, :, :]
        bi = bb_im[direction].transpose(0, 2, 1)[:, None, :, :]
        return pr * br - pi * bi, pr * bi + pi * br

    steps = jnp.arange(ell)
    f_re, f_im = state_in(0, ell - 1 - steps)
    g_re, g_im = state_in(1, steps)
    w_in = jnp.stack([f_re, f_im, g_re, g_im], axis=3)
    w_in = w_in.reshape(n_q, gl // 2, 2, ell, n_h, 4, n_p)
    w_in_c = jnp.einsum('qrsaikp,spl->qarskil', w_in, put_p, precision=hp)
    w_in_c = w_in_c.reshape(n_q, ell, gl, 4, n_h, LANES)

    def state_out(direction, powers):
        wr = cl_re[powers, direction].transpose(1, 3, 0, 2)
        wi = -cl_im[powers, direction].transpose(1, 3, 0, 2)
        w = jnp.stack([wr, wi], axis=1).reshape(n_q, gl, 2, n_p, ell, n_h)
        return jnp.einsum('qgepth,ghl->qgetpl', w, put_h, precision=hp)

    w_of_c = state_out(0, steps + 1)
    w_ob_c = state_out(1, ell - steps)
    lam_l = jnp.stack([lp_re[ell, 0], lp_im[ell, 0], lp_re[ell, 1], lp_im[ell, 1]], axis=0)
    lam_l = lam_l.reshape(4, n_g // 2, 2 * n_p)
    d_q = jnp.broadcast_to(d_skip.reshape(n_q, 1, gl, n_h), (n_q, ell, gl, n_h)).reshape(n_q, 1, ell * LANES)
    return toep_c.astype(BF16), w_in_c.astype(BF16), w_of_c.astype(BF16), w_ob_c.astype(BF16), lam_l, d_q


def _s5_build_toeplitz(toep_ref, t_ref):
    ell = S5_CHUNK
    for a in range(ell):
        for b in range(ell):
            t_ref[a * LANES:(a + 1) * LANES, b * LANES:(b + 1) * LANES] = toep_ref[b - a + ell - 1]


def _s5_build_state_in(w_ref, t_ref):
    _, gl, kinds, n_h, _ = w_ref.shape
    t_ref[...] = jnp.zeros(t_ref.shape, t_ref.dtype)
    for a in range(S5_CHUNK):
        for g in range(gl):
            r0 = (a * gl + g) * n_h
            for k in range(kinds):
                c0 = ((g // 2) * kinds + k) * LANES
                t_ref[r0:r0 + n_h, c0:c0 + LANES] = w_ref[a, g, k]


def _s5_build_state_out(w_ref, t_ref):
    gl, _, _, n_p, _ = w_ref.shape
    for g in range(gl):
        for e in range(2):
            r0 = ((g // 2) * 4 + e * 2 + g % 2) * n_p
            for t in range(S5_CHUNK):
                t_ref[r0:r0 + n_p, t * LANES:(t + 1) * LANES] = w_ref[g, e, t]


def _s5_chunk_rows(u_ref, n_rows):
    return jnp.concatenate([u_ref[pl.ds(t, n_rows, stride=S5_CHUNK), :] for t in range(S5_CHUNK)], axis=1)


def _s5_in_body(u_ref, w_ref, x_ref, win_ref):
    @pl.when(pl.program_id(1) == 0)
    def _():
        _s5_build_state_in(w_ref, win_ref)

    uc = _s5_chunk_rows(u_ref, x_ref.shape[0]).astype(BF16)
    x_ref[...] = jnp.dot(uc, win_ref[...], preferred_element_type=F32)


def _s5_scan_body(xf_ref, xb_ref, lam_ref, sf_ref, sb_ref, st_ref):
    @pl.when(pl.program_id(0) == 0)
    def _():
        st_ref[...] = jnp.zeros(st_ref.shape, F32)

    half = xf_ref.shape[2] // 2
    rows = xf_ref.shape[0]

    def run(x_ref, s_ref, base, order):
        ar, ai = lam_ref[base], lam_ref[base + 1]
        sr, si = st_ref[base], st_ref[base + 1]
        for r in order:
            s_ref[r, :, 0:half] = sr
            s_ref[r, :, half:2 * half] = si
            xr, xi = x_ref[r, :, 0:half], x_ref[r, :, half:2 * half]
            sr, si = ar * sr - ai * si + xr, ar * si + ai * sr + xi
        st_ref[base] = sr
        st_ref[base + 1] = si

    run(xf_ref, sf_ref, 0, range(rows))
    run(xb_ref, sb_ref, 2, range(rows - 1, -1, -1))


def _s5_out_body(u_ref, sf_ref, sb_ref, toep_ref, wof_ref, wob_ref, d_ref, y_ref, t_ref, of_ref, ob_ref):
    @pl.when(pl.program_id(1) == 0)
    def _():
        _s5_build_toeplitz(toep_ref, t_ref)
        _s5_build_state_out(wof_ref, of_ref)
        _s5_build_state_out(wob_ref, ob_ref)

    n_rows = sf_ref.shape[0]
    uc = _s5_chunk_rows(u_ref, n_rows)
    y = jnp.dot(uc.astype(BF16), t_ref[...], preferred_element_type=F32)
    y += jnp.dot(sf_ref[...].astype(BF16), of_ref[...], preferred_element_type=F32)
    y += jnp.dot(sb_ref[...].astype(BF16), ob_ref[...], preferred_element_type=F32)
    y += uc * d_ref[...]
    for t in range(S5_CHUNK):
        y_ref[pl.ds(t, n_rows, stride=S5_CHUNK), :] = y[:, t * LANES:(t + 1) * LANES]


def _s5(u, n_ctx, weights):
    toep_c, w_in_c, w_of_c, w_ob_c, lam_l, d_q = weights
    nt, width = u.shape
    ell = S5_CHUNK
    n_q, _, gl, kinds, _, _ = w_in_c.shape
    kdim = ell * LANES
    n_pair = lam_l.shape[1]
    sdim = kinds * LANES
    sdim_q = (gl // 2) * sdim
    n_chunk = nt // ell
    cb = _pick(n_chunk, (208, 240, 80))
    u_spec = pl.BlockSpec((cb * ell, LANES), lambda q, i: (i, q))

    def per_q(a):
        return pl.BlockSpec((None,) + a.shape[1:], lambda q, i: (q,) + (0,) * (a.ndim - 1))

    x = pl.pallas_call(
        _s5_in_body,
        grid=(n_q, n_chunk // cb),
        in_specs=[u_spec, per_q(w_in_c)],
        out_specs=pl.BlockSpec((cb, sdim_q), lambda q, i: (i, q)),
        out_shape=jax.ShapeDtypeStruct((n_chunk, n_q * sdim_q), F32),
        scratch_shapes=[pltpu.VMEM((kdim, sdim_q), BF16)],
        compiler_params=_cp(("arbitrary", "arbitrary"), 40),
        name="s5_state_in",
    )(u, w_in_c)
    xt = x.reshape(n_chunk, n_pair, sdim)
    rb = n_ctx // ell
    nb = n_chunk // rb
    half = sdim // 2
    bwd = lambda b: (jnp.where(b == 0, 0, nb - b), 0, 0)
    sf, sb = pl.pallas_call(
        _s5_scan_body,
        grid=(nb,),
        in_specs=[pl.BlockSpec((rb, n_pair, half), lambda b: (b, 0, 0)),
                  pl.BlockSpec((rb, n_pair, half), lambda b: (jnp.where(b == 0, 0, nb - b), 0, 1)),
                  pl.BlockSpec((4, n_pair, half // 2), lambda b: (0, 0, 0))],
        out_specs=[pl.BlockSpec((rb, n_pair, half), lambda b: (b, 0, 0)),
                   pl.BlockSpec((rb, n_pair, half), bwd)],
        out_shape=[jax.ShapeDtypeStruct((n_chunk, n_pair, half), F32),
                   jax.ShapeDtypeStruct((n_chunk, n_pair, half), F32)],
        scratch_shapes=[pltpu.VMEM((4, n_pair, half // 2), F32)],
        compiler_params=_cp(("arbitrary",), 32),
        name="s5_scan",
    )(xt, xt, lam_l)
    sdim_h = sdim_q // 2
    s_spec = pl.BlockSpec((cb, sdim_h), lambda q, i: (i, q))
    toep_spec = pl.BlockSpec((toep_c.shape[0], None, LANES, LANES), lambda q, i: (0, q, 0, 0))
    return pl.pallas_call(
        _s5_out_body,
        grid=(n_q, n_chunk // cb),
        in_specs=[u_spec, s_spec, s_spec, toep_spec, per_q(w_of_c), per_q(w_ob_c), per_q(d_q)],
        out_specs=u_spec,
        out_shape=jax.ShapeDtypeStruct((nt, width), F32),
        scratch_shapes=[pltpu.VMEM((kdim, kdim), BF16), pltpu.VMEM((sdim_h, kdim), BF16),
                        pltpu.VMEM((sdim_h, kdim), BF16)],
        compiler_params=_cp(("arbitrary", "arbitrary"), 56),
        name="s5_out",
    )(u, sf.reshape(n_chunk, n_pair * half), sb.reshape(n_chunk, n_pair * half), toep_c, w_of_c, w_ob_c, d_q)


def _glu_body(y_ref, w_ref, b_ref, o_ref):
    y = y_ref[...]
    gy = 0.5 * y * (1.0 + jnp.tanh(math.sqrt(2.0 / math.pi) * (y + 0.044715 * (y * y * y))))
    z = jnp.dot(gy.astype(BF16), w_ref[...], preferred_element_type=F32) + b_ref[...]
    o_ref[...] = (gy * jax.nn.sigmoid(z)).astype(BF16)


def _s5_glu(y, glu_w, glu_b):
    nt, width = y.shape
    tm = _pick(nt, (1280, 640, 256))
    return pl.pallas_call(
        _glu_body,
        grid=(nt // tm,),
        in_specs=[pl.BlockSpec((tm, width), lambda i: (i, 0)),
                  pl.BlockSpec((width, width), lambda i: (0, 0)),
                  pl.BlockSpec((1, width), lambda i: (0, 0))],
        out_specs=pl.BlockSpec((tm, width), lambda i: (i, 0)),
        out_shape=jax.ShapeDtypeStruct((nt, width), BF16),
        compiler_params=_cp(("parallel",), 40),
        name="s5_glu",
    )(y, glu_w, glu_b.reshape(1, width))


def _rope_tables(seq, n_ctx):
    t = jnp.arange(seq)
    row_id = (t // GRID_W).astype(F32)
    col_id = (t % GRID_W).astype(F32)
    n_freq = HEAD_DIM // 4
    inv = ROPE_THETA ** (-jnp.arange(n_freq, dtype=F32) / n_freq)
    ang_r = row_id[:, None] * inv[None, :]
    ang_c = col_id[:, None] * inv[None, :]
    cos_t = jnp.concatenate([jnp.cos(ang_r), jnp.cos(ang_r), jnp.cos(ang_c), jnp.cos(ang_c)], axis=1)
    sin_t = jnp.concatenate([-jnp.sin(ang_r), jnp.sin(ang_r), -jnp.sin(ang_c), jnp.sin(ang_c)], axis=1)
    cos_t = jnp.concatenate([jnp.ones((n_ctx, HEAD_DIM), F32), cos_t], axis=0)
    sin_t = jnp.concatenate([jnp.zeros((n_ctx, HEAD_DIM), F32), sin_t], axis=0)
    return cos_t, sin_t


def kernel(x, c, ctx, c_ctx, w_mod, b_mod, norm_g, ab_w_in, ab_w_out, s5_lam_re, s5_lam_im, s5_log_dt, s5_b_re, s5_b_im, s5_c_re, s5_c_im, s5_d, glu_w, glu_b, attn_sink, ffn_w1, ffn_w3, ffn_w2, c_w_in, c_w_out, q_norm, k_norm, router_w, moe_w1, moe_w3, moe_w2):
    assert x.shape[0] == 1 and ctx.shape[0] == 1
    seq, d = x.shape[1], x.shape[2]
    n_ctx = ctx.shape[1]
    assert n_ctx == ROW_TILE
    depth = w_mod.shape[0]
    s5_width = s5_d.shape[1] * s5_d.shape[2]
    b_heads = attn_sink.shape[1]
    b_kv = (ab_w_in.shape[2] - s5_width - b_heads * HEAD_DIM) // (2 * HEAD_DIM)
    c_heads = c_w_out.shape[1] // HEAD_DIM
    c_kv = (c_w_in.shape[2] - c_heads * HEAD_DIM) // (2 * HEAD_DIM)
    q_scale = HEAD_DIM ** -0.5 * LOG2E

    cond8 = jnp.zeros((8, d), F32).at[0].set(c_ctx).at[1].set(c[0])
    mod4 = _adaln(cond8, w_mod, b_mod).reshape(depth, 8, N_MOD, d)
    cos_t, sin_t = _rope_tables(seq, n_ctx)
    xa, h = _prenorm(x[0], ctx[0], norm_g, mod4)

    def col_vec(pieces):
        return jnp.concatenate([jnp.broadcast_to(jnp.asarray(v, F32), (n,)) for v, n in pieces]).reshape(1, -1)

    for i in range(depth):
        j = i // 2
        if i % 2 == 0:
            q0 = s5_width
            k0 = q0 + b_heads * HEAD_DIM
            v0 = k0 + b_kv * HEAD_DIM
            n_in = v0 + b_kv * HEAD_DIM
            proj = _inproj(h, ab_w_in[j].astype(BF16), cos_t, sin_t,
                           jnp.ones((1, n_in), F32),
                           col_vec([(1.0, q0), (q_scale, k0 - q0), (1.0, n_in - k0)]),
                           q0, v0, False)
            s5w = _s5_weights(s5_lam_re[j], s5_lam_im[j], s5_log_dt[j], s5_b_re[j], s5_b_im[j],
                              s5_c_re[j], s5_c_im[j], s5_d[j])
            a = _s5_glu(_s5(proj[:, :s5_width].astype(F32), n_ctx, s5w), glu_w[j].astype(BF16), glu_b[j])
            o = _window_attention(proj, attn_sink[j], n_ctx, q0, k0, v0, b_kv, b_heads // b_kv)
            w_out = ab_w_out[j].astype(BF16)
            y = _matmul([a, o], [w_out[:s5_width], w_out[s5_width:]], (1280, 640, 256), 512, F32, 48, "ab_out")
        else:
            k0 = c_heads * HEAD_DIM
            v0 = k0 + c_kv * HEAD_DIM
            n_in = v0 + c_kv * HEAD_DIM
            proj = _inproj(h, c_w_in[j].astype(BF16), cos_t, sin_t,
                           col_vec([(jnp.tile(q_norm[j], c_heads), k0), (jnp.tile(k_norm[j], c_kv), v0 - k0),
                                    (1.0, n_in - v0)]),
                           col_vec([(q_scale, k0), (1.0, n_in - k0)]),
                           0, v0, True)
            o = _flash_attention(proj, n_ctx, 0, k0, v0, c_kv, c_heads // c_kv)
            y = _matmul([o], [c_w_out[j].astype(BF16)], (1280, 640, 256), 512, F32, 48, "c_out")
        xa, h = _resid(xa, y, norm_g, mod4, i, 1, 2, i, 2, 3)
        if i % 2 == 0:
            g = _swiglu_up(h, ffn_w1[j].astype(BF16)[None], ffn_w3[j].astype(BF16)[None])
            f = _matmul([g], [ffn_w2[j].astype(BF16)], (1280, 640, 256), 512, F32, 56, "ffn_down")
        else:
            gates = _router(h, router_w[j])
            g = _swiglu_up(h, moe_w1[j].astype(BF16), moe_w3[j].astype(BF16), gates)
            w2 = moe_w2[j].astype(BF16).reshape(-1, d)
            f = _matmul([g], [w2], (640, 256), 512, F32, 56, "moe_down")
        if i + 1 < depth:
            xa, h = _resid(xa, f, norm_g, mod4, i, 3, 5, i + 1, 0, 0)
        else:
            out = _final(xa, f, norm_g, mod4, i, 3, 5, n_ctx)
    return out[None]
```

```python
import functools
import math

import jax
import jax.numpy as jnp
from jax import lax
from jax.experimental import pallas as pl
from jax.experimental.pallas import tpu as pltpu

F32 = jnp.float32
BF16 = jnp.bfloat16

HEAD_DIM = 128
GRID_W = 64
ROPE_THETA = 10000.0
NORM_EPS = 1e-6
NEG = -1e30
N_MOD = 6
S5_GROUP = 16
S5_STATE = 64
S5_CHUNK = 16
WINDOW = 128
TOP_K = 2
LANES = 128
ROW_TILE = 256
MIB = 1024 * 1024


def _cp(sem, vmem_mib):
    return pltpu.CompilerParams(dimension_semantics=sem, vmem_limit_bytes=vmem_mib * MIB)


def _pick(n, cands):
    for t in cands:
        if n % t == 0:
            return t
    raise ValueError(f"no tile for {n} in {cands}")


def _rms(v):
    return v * lax.rsqrt(jnp.mean(v * v, axis=-1, keepdims=True) + NORM_EPS)


def _adaln_body(cond_ref, w_ref, b_ref, o_ref):
    c = cond_ref[...]
    a = (c * jax.nn.sigmoid(c)).astype(BF16)
    o_ref[...] = jnp.dot(a, w_ref[...].astype(BF16), preferred_element_type=F32) + b_ref[...]


def _adaln(cond8, w_mod, b_mod):
    depth, d, n6 = w_mod.shape
    tn = 512
    return pl.pallas_call(
        _adaln_body,
        grid=(depth, n6 // tn),
        in_specs=[pl.BlockSpec((8, d), lambda l, j: (0, 0)),
                  pl.BlockSpec((None, d, tn), lambda l, j: (l, 0, j)),
                  pl.BlockSpec((None, 1, tn), lambda l, j: (l, 0, j))],
        out_specs=pl.BlockSpec((None, 8, tn), lambda l, j: (l, 0, j)),
        out_shape=jax.ShapeDtypeStruct((depth, 8, n6), F32),
        compiler_params=_cp(("parallel", "parallel"), 40),
        name="adaln",
    )(cond8, w_mod, b_mod.reshape(depth, 1, n6))


def _prenorm_body(x_ref, ctx_ref, g_ref, mod_ref, xa_ref, h_ref):
    i = pl.program_id(0)

    def emit(v):
        xa_ref[...] = v
        n = _rms(v) * g_ref[0:1, :]
        h_ref[...] = (n * (1.0 + mod_ref[1:2, :]) + mod_ref[0:1, :]).astype(BF16)

    @pl.when(i == 0)
    def _():
        emit(ctx_ref[...])

    @pl.when(i > 0)
    def _():
        emit(x_ref[...])


def _prenorm(x2, ctx2, norm_g, mod4):
    seq, d = x2.shape
    nc = ctx2.shape[0]
    nt = seq + nc
    tr = ROW_TILE
    return pl.pallas_call(
        _prenorm_body,
        grid=(nt // tr,),
        in_specs=[pl.BlockSpec((tr, d), lambda i: (jnp.maximum(i - 1, 0), 0)),
                  pl.BlockSpec((tr, d), lambda i: (0, 0)),
                  pl.BlockSpec((None, 4, d), lambda i: (0, 0, 0)),
                  pl.BlockSpec((None, None, N_MOD, d), lambda i: (0, jnp.minimum(i, 1), 0, 0))],
        out_specs=[pl.BlockSpec((tr, d), lambda i: (i, 0)),
                   pl.BlockSpec((tr, d), lambda i: (i, 0))],
        out_shape=[jax.ShapeDtypeStruct((nt, d), F32), jax.ShapeDtypeStruct((nt, d), BF16)],
        compiler_params=_cp(("parallel",), 40),
        name="prenorm",
    )(x2, ctx2, norm_g, mod4)


def _row_gather_start(index_of, src_hbm, dst_ref, sem):
    def issue(r, carry):
        pltpu.make_async_copy(src_hbm.at[pl.ds(index_of(r), 1), :], dst_ref.at[pl.ds(r, 1), :], sem).start()
        return carry

    lax.fori_loop(0, dst_ref.shape[0], issue, 0)


def _row_gather_wait(src_hbm, dst_ref, sem):
    def drain(r, carry):
        pltpu.make_async_copy(src_hbm.at[pl.ds(0, 1), :], dst_ref.at[pl.ds(r, 1), :], sem).wait()
        return carry

    lax.fori_loop(0, dst_ref.shape[0], drain, 0)


def _pack_halves(v):
    half = v.shape[1] // 2
    vb = v.astype(BF16).astype(F32)
    hi = lax.bitcast_convert_type(vb[:, :half], jnp.uint32)
    lo = lax.bitcast_convert_type(vb[:, half:], jnp.uint32)
    return (hi & jnp.uint32(0xFFFF0000)) | (lo >> jnp.uint32(16))


def _unpack_halves(u):
    hi = lax.bitcast_convert_type(u & jnp.uint32(0xFFFF0000), F32)
    lo = lax.bitcast_convert_type(u << jnp.uint32(16), F32)
    return jnp.concatenate([hi.astype(BF16), lo.astype(BF16)], axis=1)


def _resid_body(*refs, post_idx, gate_idx, next_g_idx, next_shift_idx, gathered, packed, last):
    refs = list(refs)
    if gathered:
        p_ref = refs.pop(0)
    xa_ref, y_ref, gcur_ref, mcur_ref = refs[:4]
    refs = refs[4:]
    if not last:
        gnext_ref, mnext_ref = refs[:2]
        refs = refs[2:]
    xo_ref = refs.pop(0)
    if gathered:
        buf_ref, sem = refs[-2:]
        _row_gather_start(lambda r: p_ref[0, r], y_ref, buf_ref.at[0], sem)
        _row_gather_start(lambda r: p_ref[1, r], y_ref, buf_ref.at[1], sem)
        _row_gather_wait(y_ref, buf_ref.at[0], sem)
        _row_gather_wait(y_ref, buf_ref.at[1], sem)
        y = buf_ref[0] + buf_ref[1]
    else:
        y = y_ref[...].astype(F32)
    x = xa_ref[...] + mcur_ref[gate_idx:gate_idx + 1, :] * (_rms(y) * gcur_ref[post_idx:post_idx + 1, :])
    xo_ref[...] = x
    if last:
        return
    n = _rms(x) * gnext_ref[next_g_idx:next_g_idx + 1, :]
    h = (n * (1.0 + mnext_ref[next_shift_idx + 1:next_shift_idx + 2, :])
         + mnext_ref[next_shift_idx:next_shift_idx + 1, :])
    refs[0][...] = h.astype(BF16)
    if packed:
        refs[1][...] = _pack_halves(h)


def _resid(xa, y, norm_g, mod4, layer, post_idx, gate_idx, next_layer=None, next_g_idx=0, next_shift_idx=0,
           rows=None, packed=False, row_offset=0):
    nt, d = xa.shape
    tr = ROW_TILE
    off = row_offset // tr
    n_blk = (nt - row_offset) // tr
    last = next_layer is None
    gathered = rows is not None
    row = lambda i: (i + off, 0)
    mod_spec = lambda l: pl.BlockSpec((None, None, N_MOD, d), lambda i: (l, jnp.minimum(i + off, 1), 0, 0))
    in_specs, args = [], []
    if gathered:
        in_specs.append(pl.BlockSpec((None, 2, tr), lambda i: (i + off, 0, 0), memory_space=pltpu.SMEM))
        args.append(rows.reshape(2, nt // tr, tr).transpose(1, 0, 2))
    in_specs += [pl.BlockSpec((tr, d), row),
                 pl.BlockSpec(memory_space=pl.ANY) if gathered else pl.BlockSpec((tr, d), row),
                 pl.BlockSpec((None, 4, d), lambda i: (layer, 0, 0)), mod_spec(layer)]
    args += [xa, y, norm_g, mod4]
    out_specs = [pl.BlockSpec((tr, d), lambda i: (i, 0) if last else (i + off, 0))]
    out_shape = [jax.ShapeDtypeStruct((nt - row_offset if last else nt, d), F32)]
    if not last:
        in_specs += [pl.BlockSpec((None, 4, d), lambda i: (next_layer, 0, 0)), mod_spec(next_layer)]
        args += [norm_g, mod4]
        out_specs.append(pl.BlockSpec((tr, d), row))
        out_shape.append(jax.ShapeDtypeStruct((nt, d), BF16))
        if packed:
            out_specs.append(pl.BlockSpec((tr, d // 2), row))
            out_shape.append(jax.ShapeDtypeStruct((nt, d // 2), jnp.uint32))
    scratch = [pltpu.VMEM((2, tr, d), F32), pltpu.SemaphoreType.DMA(())] if gathered else []
    body = functools.partial(_resid_body, post_idx=post_idx, gate_idx=gate_idx, next_g_idx=next_g_idx,
                             next_shift_idx=next_shift_idx, gathered=gathered, packed=packed, last=last)
    return pl.pallas_call(
        body,
        grid=(n_blk,),
        in_specs=in_specs,
        out_specs=out_specs,
        out_shape=out_shape,
        scratch_shapes=scratch,
        input_output_aliases={} if last else {(1 if gathered else 0): 0},
        compiler_params=_cp(("arbitrary",) if gathered else ("parallel",), 48),
        name="resid",
    )(*args)


def _inproj_body(h_ref, w_ref, cos_ref, sin_ref, gain_ref, scale_ref, o_ref, *, rope_lo, rope_hi, use_norm):
    j = pl.program_id(1)
    acc = jnp.dot(h_ref[...], w_ref[...], preferred_element_type=F32)
    is_rope = jnp.logical_and(j >= rope_lo, j < rope_hi)

    @pl.when(jnp.logical_not(is_rope))
    def _():
        o_ref[...] = acc.astype(BF16)

    @pl.when(is_rope)
    def _():
        c = cos_ref[...]
        s = sin_ref[...]
        lane = lax.broadcasted_iota(jnp.int32, c.shape, 1)
        first = (lane & 32) == 0
        for hh in range(acc.shape[1] // HEAD_DIM):
            sl = slice(hh * HEAD_DIM, (hh + 1) * HEAD_DIM)
            blk = acc[:, sl]
            if use_norm:
                blk = _rms(blk) * gain_ref[:, sl]
            rot = jnp.where(first, pltpu.roll(blk, HEAD_DIM - 32, 1), pltpu.roll(blk, 32, 1))
            o_ref[:, sl] = ((blk * c + rot * s) * scale_ref[:, sl]).astype(BF16)


def _inproj(h, w, cos_t, sin_t, gain, scale, rope_lo, rope_hi, use_norm):
    nt, d = h.shape
    n = w.shape[1]
    tm = _pick(nt, (1280, 640, 256))
    tn = 512
    body = functools.partial(_inproj_body, rope_lo=rope_lo // tn, rope_hi=rope_hi // tn, use_norm=use_norm)
    return pl.pallas_call(
        body,
        grid=(nt // tm, n // tn),
        in_specs=[pl.BlockSpec((tm, d), lambda i, j: (i, 0)),
                  pl.BlockSpec((d, tn), lambda i, j: (0, j)),
                  pl.BlockSpec((tm, HEAD_DIM), lambda i, j: (i, 0)),
                  pl.BlockSpec((tm, HEAD_DIM), lambda i, j: (i, 0)),
                  pl.BlockSpec((1, tn), lambda i, j: (0, j)),
                  pl.BlockSpec((1, tn), lambda i, j: (0, j))],
        out_specs=pl.BlockSpec((tm, tn), lambda i, j: (i, j)),
        out_shape=jax.ShapeDtypeStruct((nt, n), BF16),
        compiler_params=_cp(("parallel", "arbitrary"), 48),
        name="inproj",
    )(h, w, cos_t, sin_t, gain, scale)


def _mm_body(*refs, n_a):
    o_ref = refs[2 * n_a]
    acc = jnp.dot(refs[0][...], refs[n_a][...], preferred_element_type=F32)
    for t in range(1, n_a):
        acc += jnp.dot(refs[t][...], refs[n_a + t][...], preferred_element_type=F32)
    o_ref[...] = acc.astype(o_ref.dtype)


def _matmul(a_list, w_list, tm_cands, tn, out_dtype, vmem_mib, name):
    nt = a_list[0].shape[0]
    n = w_list[0].shape[1]
    tm = _pick(nt, tm_cands)
    in_specs = [pl.BlockSpec((tm, a.shape[1]), lambda i, j: (i, 0)) for a in a_list]
    in_specs += [pl.BlockSpec((w.shape[0], tn), lambda i, j: (0, j)) for w in w_list]
    return pl.pallas_call(
        functools.partial(_mm_body, n_a=len(a_list)),
        grid=(nt // tm, n // tn),
        in_specs=in_specs,
        out_specs=pl.BlockSpec((tm, tn), lambda i, j: (i, j)),
        out_shape=jax.ShapeDtypeStruct((nt, n), out_dtype),
        compiler_params=_cp(("parallel", "arbitrary"), vmem_mib),
        name=name,
    )(*a_list, *w_list)


def _swiglu_up_body(h_ref, w1_ref, w3_ref, o_ref):
    h = h_ref[...]
    a = jnp.dot(h, w1_ref[...], preferred_element_type=F32)
    b = jnp.dot(h, w3_ref[...], preferred_element_type=F32)
    o_ref[...] = (a * jax.nn.sigmoid(a) * b).astype(BF16)


def _swiglu_up(h, w1, w3):
    nt, d = h.shape
    f = w1.shape[1]
    tm, tn = _pick(nt, (1280, 640, 256)), 512
    w_spec = pl.BlockSpec((d, tn), lambda i, j: (0, j))
    return pl.pallas_call(
        _swiglu_up_body,
        grid=(nt // tm, f // tn),
        in_specs=[pl.BlockSpec((tm, d), lambda i, j: (i, 0)), w_spec, w_spec],
        out_specs=pl.BlockSpec((tm, tn), lambda i, j: (i, j)),
        out_shape=jax.ShapeDtypeStruct((nt, f), BF16),
        compiler_params=_cp(("parallel", "arbitrary"), 52),
        name="swiglu_up",
    )(h, w1, w3)


def _router_body(h_ref, whi_ref, wlo_ref, g_ref, *, n_experts):
    h = h_ref[...]
    logits = (jnp.dot(h, whi_ref[...], preferred_element_type=F32)
              + jnp.dot(h, wlo_ref[...], preferred_element_type=F32))
    lane = lax.broadcasted_iota(jnp.int32, logits.shape, 1)
    lg = jnp.where(lane < n_experts, logits, NEG)
    v1 = jnp.max(lg, axis=-1, keepdims=True)
    i1 = jnp.min(jnp.where(lg == v1, lane, LANES), axis=-1, keepdims=True)
    lg2 = jnp.where(lane == i1, NEG, lg)
    v2 = jnp.max(lg2, axis=-1, keepdims=True)
    i2 = jnp.min(jnp.where(lg2 == v2, lane, LANES), axis=-1, keepdims=True)
    e = jnp.exp(v2 - v1)
    den = 1.0 + e
    g_ref[...] = (jnp.where(lane == i1, 1.0 / den, 0.0) + jnp.where(lane == i2, e / den, 0.0)
                  + jnp.where(lane == n_experts, i1.astype(F32), 0.0)
                  + jnp.where(lane == n_experts + 1, i2.astype(F32), 0.0))


def _router(h, router_w):
    nt, d = h.shape
    n_e = router_w.shape[1]
    whi = router_w.astype(BF16)
    wlo = (router_w - whi.astype(F32)).astype(BF16)
    pad = ((0, 0), (0, LANES - n_e))
    whi = jnp.pad(whi, pad)
    wlo = jnp.pad(wlo, pad)
    tm = _pick(nt, (640, 256))
    return pl.pallas_call(
        functools.partial(_router_body, n_experts=n_e),
        grid=(nt // tm,),
        in_specs=[pl.BlockSpec((tm, d), lambda i: (i, 0)),
                  pl.BlockSpec((d, LANES), lambda i: (0, 0)),
                  pl.BlockSpec((d, LANES), lambda i: (0, 0))],
        out_specs=pl.BlockSpec((tm, LANES), lambda i: (i, 0)),
        out_shape=jax.ShapeDtypeStruct((nt, LANES), F32),
        compiler_params=_cp(("parallel",), 32),
        name="router",
    )(h, whi, wlo)


MOE_ROWS = 512


def _moe_plan(gates, n_e):
    nt = gates.shape[0]
    tg = MOE_ROWS
    i1 = gates[:, n_e].astype(jnp.int32)
    i2 = gates[:, n_e + 1].astype(jnp.int32)
    w1 = jnp.take_along_axis(gates, i1[:, None], axis=1)[:, 0]
    w2 = jnp.take_along_axis(gates, i2[:, None], axis=1)[:, 0]
    experts = jnp.arange(n_e, dtype=jnp.int32)
    chosen = ((i1[:, None] == experts) | (i2[:, None] == experts)).astype(jnp.int32)
    rank = jnp.cumsum(chosen, axis=0) - chosen
    padded = (jnp.sum(chosen, axis=0) + tg - 1) // tg * tg
    ends = jnp.cumsum(padded)
    pos = (ends - padded)[None, :] + rank
    p1 = jnp.take_along_axis(pos, i1[:, None], axis=1)[:, 0]
    p2 = jnp.take_along_axis(pos, i2[:, None], axis=1)[:, 0]
    n_rows = 2 * nt + n_e * tg
    token = jnp.arange(nt, dtype=jnp.int32)
    src = jnp.zeros((n_rows,), jnp.int32).at[p1].set(token).at[p2].set(token)
    wrow = jnp.zeros((n_rows,), F32).at[p1].set(w1).at[p2].set(w2)
    blk_start = jnp.arange(n_rows // tg, dtype=jnp.int32) * tg
    blk_expert = jnp.minimum(jnp.searchsorted(ends, blk_start, side='right'), n_e - 1).astype(jnp.int32)
    n_used = (ends[-1] // tg).astype(jnp.int32).reshape(1)
    return src, wrow, blk_expert, n_used, jnp.stack([p1, p2]).astype(jnp.int32)


def _dispatch_body(src_ref, hp_hbm, o_ref, sem):
    _row_gather_start(lambda r: src_ref[0, r], hp_hbm, o_ref, sem)
    _row_gather_wait(hp_hbm, o_ref, sem)


def _moe_dispatch(hp, src):
    n_rows, tg = src.shape[0], MOE_ROWS
    return pl.pallas_call(
        _dispatch_body,
        grid=(n_rows // tg,),
        in_specs=[pl.BlockSpec((None, 1, tg), lambda b: (b, 0, 0), memory_space=pltpu.SMEM),
                  pl.BlockSpec(memory_space=pl.ANY)],
        out_specs=pl.BlockSpec((tg, hp.shape[1]), lambda b: (b, 0)),
        out_shape=jax.ShapeDtypeStruct((n_rows, hp.shape[1]), jnp.uint32),
        scratch_shapes=[pltpu.SemaphoreType.DMA(())],
        compiler_params=_cp(("arbitrary",), 32),
        name="moe_dispatch",
    )(src.reshape(n_rows // tg, 1, tg), hp)


def _moe_up_body(expert_ref, used_ref, x_ref, w1_ref, w3_ref, gate_ref, o_ref):
    b = pl.program_id(0)

    @pl.when(b < used_ref[0])
    def _():
        x = _unpack_halves(x_ref[...])
        a = jnp.dot(x, w1_ref[...], preferred_element_type=F32)
        c = jnp.dot(x, w3_ref[...], preferred_element_type=F32)
        o_ref[...] = (a * jax.nn.sigmoid(a) * c * gate_ref[...]).astype(BF16)

    @pl.when(b >= used_ref[0])
    def _():
        o_ref[...] = jnp.zeros(o_ref.shape, o_ref.dtype)


def _moe_down_body(expert_ref, used_ref, g_ref, w2_ref, o_ref):
    b = pl.program_id(0)

    @pl.when(b < used_ref[0])
    def _():
        o_ref[...] = jnp.dot(g_ref[...], w2_ref[...], preferred_element_type=F32)

    @pl.when(b >= used_ref[0])
    def _():
        o_ref[...] = jnp.zeros(o_ref.shape, o_ref.dtype)


def _moe_experts(xs, wrow, blk_expert, n_used, w1, w3, w2):
    n_rows, tg = xs.shape[0], MOE_ROWS
    n_e, d, f = w1.shape
    row = lambda b, e, u: (b, 0)
    per_expert = lambda b, e, u: (e[b], 0, 0)
    g = pl.pallas_call(
        _moe_up_body,
        grid_spec=pltpu.PrefetchScalarGridSpec(
            num_scalar_prefetch=2, grid=(n_rows // tg,),
            in_specs=[pl.BlockSpec((tg, xs.shape[1]), row), pl.BlockSpec((None, d, f), per_expert),
                      pl.BlockSpec((None, d, f), per_expert), pl.BlockSpec((tg, 1), row)],
            out_specs=pl.BlockSpec((tg, f), row)),
        out_shape=jax.ShapeDtypeStruct((n_rows, f), BF16),
        compiler_params=_cp(("arbitrary",), 56),
        name="moe_up",
    )(blk_expert, n_used, xs, w1, w3, wrow.reshape(n_rows, 1))
    return pl.pallas_call(
        _moe_down_body,
        grid_spec=pltpu.PrefetchScalarGridSpec(
            num_scalar_prefetch=2, grid=(n_rows // tg,),
            in_specs=[pl.BlockSpec((tg, f), row), pl.BlockSpec((None, f, d), per_expert)],
            out_specs=pl.BlockSpec((tg, d), row)),
        out_shape=jax.ShapeDtypeStruct((n_rows, d), F32),
        compiler_params=_cp(("arbitrary",), 48),
        name="moe_down",
    )(blk_expert, n_used, g, w2)


V_ROWS = HEAD_DIM + 16
LOG2E = math.log2(math.e)


def _stack_heads(q, n_grp):
    return jnp.concatenate([q[:, g * HEAD_DIM:(g + 1) * HEAD_DIM] for g in range(n_grp)], axis=0)


def _unstack_store(o_ref, o, n_grp, rows):
    for g in range(n_grp):
        o_ref[:, g * HEAD_DIM:(g + 1) * HEAD_DIM] = o[g * rows:(g + 1) * rows, :].astype(o_ref.dtype)


def _scores_t(k, q):
    return lax.dot_general(k, q, (((1,), (1,)), ((), ())), preferred_element_type=F32)


def _transpose_v(v, n_kv):
    nt = v.shape[0]
    vt = v.reshape(nt, n_kv, HEAD_DIM).transpose(1, 2, 0)
    return jnp.concatenate([vt, jnp.ones((n_kv, V_ROWS - HEAD_DIM, nt), v.dtype)], axis=1)


def _win_body(sink_ref, q_ref, km_ref, k0_ref, kp_ref, kc_ref, vm_ref, v0_ref, vp_ref, vc_ref, o_ref, *,
              n_grp, ctx_blocks, n_lat):
    kvh = pl.program_id(0)
    n = pl.program_id(1)
    blk = q_ref.shape[0]
    qs = _stack_heads(q_ref[...], n_grp)
    keys = jnp.concatenate([km_ref[...], k0_ref[...], kp_ref[...], kc_ref[...]], axis=0)
    vt = jnp.concatenate([vm_ref[...], v0_ref[...], vp_ref[...], vc_ref[...]], axis=1)
    st = _scores_t(keys, qs)
    n_win = 3 * blk
    k_off = lax.broadcasted_iota(jnp.int32, st.shape, 0) - blk
    q_off = lax.broadcasted_iota(jnp.int32, st.shape, 1) % blk
    kpos = (n - ctx_blocks) * blk + k_off
    in_band = ((jnp.abs(q_off - k_off) <= WINDOW) & (kpos >= 0) & (kpos < n_lat) & (n >= ctx_blocks))
    st = jnp.where(in_band | (k_off >= n_win - blk), st, NEG)
    sink = jnp.concatenate(
        [jnp.full((1, blk), sink_ref[kvh * n_grp + g] * LOG2E, F32) for g in range(n_grp)], axis=1)
    m = jnp.maximum(jnp.max(st, axis=0, keepdims=True), sink)
    pt = jnp.exp2(st - m).astype(BF16)
    acc = jnp.dot(vt, pt, preferred_element_type=F32)
    den = acc[HEAD_DIM:HEAD_DIM + 1, :] + jnp.exp2(sink - m)
    _unstack_store(o_ref, (acc[0:HEAD_DIM, :] / den).T, n_grp, blk)


def _window_attention(proj, sink, n_ctx, q_col, k_col, v_col, n_kv, n_grp):
    nt = proj.shape[0]
    blk = WINDOW
    nb = nt // blk
    cb = n_ctx // blk
    qc, kc = q_col // (n_grp * HEAD_DIM), k_col // HEAD_DIM
    vt = _transpose_v(proj[:, v_col:v_col + n_kv * HEAD_DIM], n_kv)
    lo = lambda n: jnp.maximum(n - 1, cb)
    hi = lambda n: jnp.minimum(n + 1, nb - 1)
    same = lambda n: n
    k_tile = lambda rowf: pl.BlockSpec((blk, HEAD_DIM), lambda h, n: (rowf(n), kc + h))
    v_tile = lambda rowf: pl.BlockSpec((None, V_ROWS, blk), lambda h, n: (h, 0, rowf(n)))
    body = functools.partial(_win_body, n_grp=n_grp, ctx_blocks=cb, n_lat=nt - n_ctx)
    return pl.pallas_call(
        body,
        grid=(n_kv, nb),
        in_specs=[pl.BlockSpec(memory_space=pltpu.SMEM),
                  pl.BlockSpec((blk, n_grp * HEAD_DIM), lambda h, n: (n, qc + h)),
                  k_tile(lo), k_tile(same), k_tile(hi),
                  pl.BlockSpec((n_ctx, HEAD_DIM), lambda h, n: (0, kc + h)),
                  v_tile(lo), v_tile(same), v_tile(hi),
                  pl.BlockSpec((None, V_ROWS, n_ctx), lambda h, n: (h, 0, 0))],
        out_specs=pl.BlockSpec((blk, n_grp * HEAD_DIM), lambda h, n: (n, h)),
        out_shape=jax.ShapeDtypeStruct((nt, n_kv * n_grp * HEAD_DIM), BF16),
        compiler_params=_cp(("parallel", "parallel"), 32),
        name="window_attn",
    )(sink, proj, proj, proj, proj, proj, vt, vt, vt, vt)


def _flash_body(q_ref, k_ref, vt_ref, o_ref, s_ref, acc_ref, *, n_grp, tk, n_chunks, n_ctx):
    qi = pl.program_id(1)
    tq = q_ref.shape[0]
    qs = _stack_heads(q_ref[...], n_grp)
    is_ctx = qi == 0

    def scores(c):
        start = c * tk if isinstance(c, int) else pl.multiple_of(c * tk, tk)
        return _scores_t(k_ref[pl.ds(start, tk), :], qs)

    def absorb(c, slot, m_old):
        m_new = jnp.maximum(m_old, jnp.max(s_ref[slot], axis=0, keepdims=True))
        alpha = jnp.exp2(m_old - m_new)
        pt = jnp.exp2(s_ref[slot] - m_new).astype(BF16)
        acc_ref[...] = alpha * acc_ref[...] + jnp.dot(vt_ref[c], pt, preferred_element_type=F32)
        return m_new

    acc_ref[...] = jnp.zeros(acc_ref.shape, F32)
    st0 = scores(0)
    row = lax.broadcasted_iota(jnp.int32, st0.shape, 0)
    s_ref[0] = jnp.where(row < jnp.where(is_ctx, n_ctx, tk), st0, NEG)
    m = jnp.full((1, qs.shape[0]), NEG, F32)

    def pair(j, m):
        c = 2 * j
        s_ref[1] = scores(c + 1)
        m = absorb(c, 0, m)
        s_ref[0] = scores(c + 2)
        return absorb(c + 1, 1, m)

    n_pairs = (n_chunks - 1) // 2
    m = lax.fori_loop(0, jnp.where(is_ctx, 0, n_pairs), pair, m)
    if (n_chunks - 1) % 2 == 0:
        absorb(jnp.where(is_ctx, 0, n_chunks - 1), 0, m)
    else:
        s_ref[1] = jnp.where(is_ctx, NEG, scores(n_chunks - 1))
        m = absorb(jnp.where(is_ctx, 0, n_chunks - 2), 0, m)
        absorb(n_chunks - 1, 1, m)
    acc = acc_ref[...]
    _unstack_store(o_ref, (acc[0:HEAD_DIM, :] / acc[HEAD_DIM:HEAD_DIM + 1, :]).T, n_grp, tq)


def _flash_attention(proj, n_ctx, q_col, k_col, v_col, n_kv, n_grp):
    nt = proj.shape[0]
    tq = n_ctx
    tk = _pick(nt, (1280, 256))
    n_chunks = nt // tk
    qc, kc = q_col // (n_grp * HEAD_DIM), k_col // HEAD_DIM
    vt = _transpose_v(proj[:, v_col:v_col + n_kv * HEAD_DIM], n_kv)
    vt = vt.reshape(n_kv, V_ROWS, n_chunks, tk).transpose(0, 2, 1, 3)
    body = functools.partial(_flash_body, n_grp=n_grp, tk=tk, n_chunks=n_chunks, n_ctx=n_ctx)
    return pl.pallas_call(
        body,
        grid=(n_kv, nt // tq),
        in_specs=[pl.BlockSpec((tq, n_grp * HEAD_DIM), lambda h, i: (i, qc + h)),
                  pl.BlockSpec((nt, HEAD_DIM), lambda h, i: (0, kc + h)),
                  pl.BlockSpec((None, n_chunks, V_ROWS, tk), lambda h, i: (h, 0, 0, 0))],
        out_specs=pl.BlockSpec((tq, n_grp * HEAD_DIM), lambda h, i: (i, h)),
        out_shape=jax.ShapeDtypeStruct((nt, n_kv * n_grp * HEAD_DIM), BF16),
        scratch_shapes=[pltpu.VMEM((2, tk, n_grp * tq), F32), pltpu.VMEM((V_ROWS, n_grp * tq), F32)],
        compiler_params=_cp(("parallel", "parallel"), 56),
        name="flash_attn",
    )(proj, proj, vt)


def _s5_weights(lam_re, lam_im, log_dt, b_re, b_im, c_re, c_im, d_skip):
    hp = lax.Precision.HIGHEST
    n_g, n_p = lam_re.shape[1], lam_re.shape[2]
    n_h = b_re.shape[-1]
    ell = S5_CHUNK
    dt = jnp.exp(log_dt)[..., None]
    ar, ai = lam_re * dt, lam_im * dt
    k = jnp.arange(ell + 1, dtype=F32)[:, None, None, None]
    mag = jnp.exp(k * ar[None])
    lp_re, lp_im = mag * jnp.cos(k * ai[None]), mag * jnp.sin(k * ai[None])
    nr, ni = lp_re[1] - 1.0, lp_im[1]
    den = lam_re * lam_re + lam_im * lam_im
    fr, fi = (nr * lam_re + ni * lam_im) / den, (ni * lam_re - nr * lam_im) / den
    bb_re = fr[..., None] * b_re - fi[..., None] * b_im
    bb_im = fr[..., None] * b_im + fi[..., None] * b_re
    lpe_re, lpe_im = lp_re[:, :, :, None, :], lp_im[:, :, :, None, :]
    cl_re = c_re[None] * lpe_re - c_im[None] * lpe_im
    cl_im = c_re[None] * lpe_im + c_im[None] * lpe_re
    m = (jnp.einsum('kdghp,dgpi->kdghi', cl_re, bb_re, precision=hp)
         - jnp.einsum('kdghp,dgpi->kdghi', cl_im, bb_im, precision=hp))
    gl = LANES // n_h
    n_q = n_g // gl
    lane = jnp.arange(LANES)
    put_h = (lane[None, None, :] == (jnp.arange(gl)[:, None, None] * n_h
                                     + jnp.arange(n_h)[None, :, None])).astype(F32)
    put_p = (lane[None, None, :] == (jnp.arange(2)[:, None, None] * n_p
                                     + jnp.arange(n_p)[None, :, None])).astype(F32)
    m_lag = jnp.concatenate([m[ell - 1:0:-1, 1], (m[0, 0] + m[0, 1])[None], m[1:ell, 0]], axis=0)
    m_lag = m_lag.reshape(2 * ell - 1, n_q, gl, n_h, n_h)
    toep_c = jnp.einsum('dqghi,gir,ghl->dqrl', m_lag, put_h, put_h, precision=hp)

    def state_in(direction, powers):
        pr = lp_re[powers, direction].transpose(1, 0, 2)[:, :, None, :]
        pi = lp_im[powers, direction].transpose(1, 0, 2)[:, :, None, :]
        br = bb_re[direction].transpose(0, 2, 1)[:, None, :, :]
        bi = bb_im[direction].transpose(0, 2, 1)[:, None, :, :]
        return pr * br - pi * bi, pr * bi + pi * br

    steps = jnp.arange(ell)
    f_re, f_im = state_in(0, ell - 1 - steps)
    g_re, g_im = state_in(1, steps)
    w_in = jnp.stack([f_re, f_im, g_re, g_im], axis=3)
    w_in = w_in.reshape(n_q, gl // 2, 2, ell, n_h, 4, n_p)
    w_in_c = jnp.einsum('qrsaikp,spl->qarskil', w_in, put_p, precision=hp)
    w_in_c = w_in_c.reshape(n_q, ell, gl, 4, n_h, LANES)

    def state_out(direction, powers):
        wr = cl_re[powers, direction].transpose(1, 3, 0, 2)
        wi = -cl_im[powers, direction].transpose(1, 3, 0, 2)
        w = jnp.stack([wr, wi], axis=1).reshape(n_q, gl, 2, n_p, ell, n_h)
        return jnp.einsum('qgepth,ghl->qgetpl', w, put_h, precision=hp)

    w_of_c = state_out(0, steps + 1)
    w_ob_c = state_out(1, ell - steps)
    lam_l = jnp.stack([lp_re[ell, 0], lp_im[ell, 0], lp_re[ell, 1], lp_im[ell, 1]], axis=0)
    lam_l = lam_l.reshape(4, n_g // 2, 2 * n_p)
    d_q = jnp.broadcast_to(d_skip.reshape(n_q, 1, gl, n_h), (n_q, ell, gl, n_h)).reshape(n_q, 1, ell * LANES)
    return toep_c.astype(BF16), w_in_c.astype(BF16), w_of_c.astype(BF16), w_ob_c.astype(BF16), lam_l, d_q


def _s5_build_toeplitz(toep_ref, t_ref):
    ell = S5_CHUNK
    for a in range(ell):
        for b in range(ell):
            t_ref[a * LANES:(a + 1) * LANES, b * LANES:(b + 1) * LANES] = toep_ref[b - a + ell - 1]


def _s5_build_state_in(w_ref, t_ref):
    _, gl, kinds, n_h, _ = w_ref.shape
    t_ref[...] = jnp.zeros(t_ref.shape, t_ref.dtype)
    for a in range(S5_CHUNK):
        for g in range(gl):
            r0 = (a * gl + g) * n_h
            for k in range(kinds):
                c0 = ((g // 2) * kinds + k) * LANES
                t_ref[r0:r0 + n_h, c0:c0 + LANES] = w_ref[a, g, k]


def _s5_build_state_out(w_ref, t_ref):
    gl, _, _, n_p, _ = w_ref.shape
    for g in range(gl):
        for e in range(2):
            r0 = ((g // 2) * 4 + e * 2 + g % 2) * n_p
            for t in range(S5_CHUNK):
                t_ref[r0:r0 + n_p, t * LANES:(t + 1) * LANES] = w_ref[g, e, t]


def _s5_chunk_rows(u_ref, n_rows):
    return jnp.concatenate([u_ref[pl.ds(t, n_rows, stride=S5_CHUNK), :] for t in range(S5_CHUNK)], axis=1)


def _s5_in_body(u_ref, w_ref, x_ref, win_ref):
    @pl.when(pl.program_id(1) == 0)
    def _():
        _s5_build_state_in(w_ref, win_ref)

    uc = _s5_chunk_rows(u_ref, x_ref.shape[0]).astype(BF16)
    x_ref[...] = jnp.dot(uc, win_ref[...], preferred_element_type=F32)


def _s5_scan_body(xf_ref, xb_ref, lam_ref, sf_ref, sb_ref, st_ref):
    @pl.when(pl.program_id(0) == 0)
    def _():
        st_ref[...] = jnp.zeros(st_ref.shape, F32)

    half = xf_ref.shape[2] // 2
    rows = xf_ref.shape[0]

    def run(x_ref, s_ref, base, order):
        ar, ai = lam_ref[base], lam_ref[base + 1]
        sr, si = st_ref[base], st_ref[base + 1]
        for r in order:
            s_ref[r, :, 0:half] = sr
            s_ref[r, :, half:2 * half] = si
            xr, xi = x_ref[r, :, 0:half], x_ref[r, :, half:2 * half]
            sr, si = ar * sr - ai * si + xr, ar * si + ai * sr + xi
        st_ref[base] = sr
        st_ref[base + 1] = si

    run(xf_ref, sf_ref, 0, range(rows))
    run(xb_ref, sb_ref, 2, range(rows - 1, -1, -1))


def _s5_out_body(u_ref, sf_ref, sb_ref, toep_ref, wof_ref, wob_ref, d_ref, y_ref, t_ref, of_ref, ob_ref):
    @pl.when(pl.program_id(1) == 0)
    def _():
        _s5_build_toeplitz(toep_ref, t_ref)
        _s5_build_state_out(wof_ref, of_ref)
        _s5_build_state_out(wob_ref, ob_ref)

    n_rows = sf_ref.shape[0]
    uc = _s5_chunk_rows(u_ref, n_rows)
    y = jnp.dot(uc.astype(BF16), t_ref[...], preferred_element_type=F32)
    y += jnp.dot(sf_ref[...].astype(BF16), of_ref[...], preferred_element_type=F32)
    y += jnp.dot(sb_ref[...].astype(BF16), ob_ref[...], preferred_element_type=F32)
    y += uc * d_ref[...]
    for t in range(S5_CHUNK):
        y_ref[pl.ds(t, n_rows, stride=S5_CHUNK), :] = y[:, t * LANES:(t + 1) * LANES]


def _s5(u, n_ctx, weights):
    toep_c, w_in_c, w_of_c, w_ob_c, lam_l, d_q = weights
    nt, width = u.shape
    ell = S5_CHUNK
    n_q, _, gl, kinds, _, _ = w_in_c.shape
    kdim = ell * LANES
    n_pair = lam_l.shape[1]
    sdim = kinds * LANES
    sdim_q = (gl // 2) * sdim
    n_chunk = nt // ell
    cb = _pick(n_chunk, (208, 240, 80))
    u_spec = pl.BlockSpec((cb * ell, LANES), lambda q, i: (i, q))

    def per_q(a):
        return pl.BlockSpec((None,) + a.shape[1:], lambda q, i: (q,) + (0,) * (a.ndim - 1))

    x = pl.pallas_call(
        _s5_in_body,
        grid=(n_q, n_chunk // cb),
        in_specs=[u_spec, per_q(w_in_c)],
        out_specs=pl.BlockSpec((cb, sdim_q), lambda q, i: (i, q)),
        out_shape=jax.ShapeDtypeStruct((n_chunk, n_q * sdim_q), F32),
        scratch_shapes=[pltpu.VMEM((kdim, sdim_q), BF16)],
        compiler_params=_cp(("arbitrary", "arbitrary"), 40),
        name="s5_state_in",
    )(u, w_in_c)
    xt = x.reshape(n_chunk, n_pair, sdim)
    rb = n_ctx // ell
    nb = n_chunk // rb
    half = sdim // 2
    bwd = lambda b: (jnp.where(b == 0, 0, nb - b), 0, 0)
    sf, sb = pl.pallas_call(
        _s5_scan_body,
        grid=(nb,),
        in_specs=[pl.BlockSpec((rb, n_pair, half), lambda b: (b, 0, 0)),
                  pl.BlockSpec((rb, n_pair, half), lambda b: (jnp.where(b == 0, 0, nb - b), 0, 1)),
                  pl.BlockSpec((4, n_pair, half // 2), lambda b: (0, 0, 0))],
        out_specs=[pl.BlockSpec((rb, n_pair, half), lambda b: (b, 0, 0)),
                   pl.BlockSpec((rb, n_pair, half), bwd)],
        out_shape=[jax.ShapeDtypeStruct((n_chunk, n_pair, half), F32),
                   jax.ShapeDtypeStruct((n_chunk, n_pair, half), F32)],
        scratch_shapes=[pltpu.VMEM((4, n_pair, half // 2), F32)],
        compiler_params=_cp(("arbitrary",), 32),
        name="s5_scan",
    )(xt, xt, lam_l)
    sdim_h = sdim_q // 2
    s_spec = pl.BlockSpec((cb, sdim_h), lambda q, i: (i, q))
    toep_spec = pl.BlockSpec((toep_c.shape[0], None, LANES, LANES), lambda q, i: (0, q, 0, 0))
    return pl.pallas_call(
        _s5_out_body,
        grid=(n_q, n_chunk // cb),
        in_specs=[u_spec, s_spec, s_spec, toep_spec, per_q(w_of_c), per_q(w_ob_c), per_q(d_q)],
        out_specs=u_spec,
        out_shape=jax.ShapeDtypeStruct((nt, width), F32),
        scratch_shapes=[pltpu.VMEM((kdim, kdim), BF16), pltpu.VMEM((sdim_h, kdim), BF16),
                        pltpu.VMEM((sdim_h, kdim), BF16)],
        compiler_params=_cp(("arbitrary", "arbitrary"), 56),
        name="s5_out",
    )(u, sf.reshape(n_chunk, n_pair * half), sb.reshape(n_chunk, n_pair * half), toep_c, w_of_c, w_ob_c, d_q)


def _glu_body(y_ref, w_ref, b_ref, o_ref):
    y = y_ref[...]
    gy = 0.5 * y * (1.0 + jnp.tanh(math.sqrt(2.0 / math.pi) * (y + 0.044715 * (y * y * y))))
    z = jnp.dot(gy.astype(BF16), w_ref[...], preferred_element_type=F32) + b_ref[...]
    o_ref[...] = (gy * jax.nn.sigmoid(z)).astype(BF16)


def _s5_glu(y, glu_w, glu_b):
    nt, width = y.shape
    tm = _pick(nt, (1280, 640, 256))
    return pl.pallas_call(
        _glu_body,
        grid=(nt // tm,),
        in_specs=[pl.BlockSpec((tm, width), lambda i: (i, 0)),
                  pl.BlockSpec((width, width), lambda i: (0, 0)),
                  pl.BlockSpec((1, width), lambda i: (0, 0))],
        out_specs=pl.BlockSpec((tm, width), lambda i: (i, 0)),
        out_shape=jax.ShapeDtypeStruct((nt, width), BF16),
        compiler_params=_cp(("parallel",), 40),
        name="s5_glu",
    )(y, glu_w, glu_b.reshape(1, width))


def _rope_tables(seq, n_ctx):
    t = jnp.arange(seq)
    row_id = (t // GRID_W).astype(F32)
    col_id = (t % GRID_W).astype(F32)
    n_freq = HEAD_DIM // 4
    inv = ROPE_THETA ** (-jnp.arange(n_freq, dtype=F32) / n_freq)
    ang_r = row_id[:, None] * inv[None, :]
    ang_c = col_id[:, None] * inv[None, :]
    cos_t = jnp.concatenate([jnp.cos(ang_r), jnp.cos(ang_r), jnp.cos(ang_c), jnp.cos(ang_c)], axis=1)
    sin_t = jnp.concatenate([-jnp.sin(ang_r), jnp.sin(ang_r), -jnp.sin(ang_c), jnp.sin(ang_c)], axis=1)
    cos_t = jnp.concatenate([jnp.ones((n_ctx, HEAD_DIM), F32), cos_t], axis=0)
    sin_t = jnp.concatenate([jnp.zeros((n_ctx, HEAD_DIM), F32), sin_t], axis=0)
    return cos_t, sin_t


def kernel(x, c, ctx, c_ctx, w_mod, b_mod, norm_g, ab_w_in, ab_w_out, s5_lam_re, s5_lam_im, s5_log_dt, s5_b_re, s5_b_im, s5_c_re, s5_c_im, s5_d, glu_w, glu_b, attn_sink, ffn_w1, ffn_w3, ffn_w2, c_w_in, c_w_out, q_norm, k_norm, router_w, moe_w1, moe_w3, moe_w2):
    assert x.shape[0] == 1 and ctx.shape[0] == 1
    seq, d = x.shape[1], x.shape[2]
    n_ctx = ctx.shape[1]
    assert n_ctx == ROW_TILE
    depth = w_mod.shape[0]
    s5_width = s5_d.shape[1] * s5_d.shape[2]
    b_heads = attn_sink.shape[1]
    b_kv = (ab_w_in.shape[2] - s5_width - b_heads * HEAD_DIM) // (2 * HEAD_DIM)
    c_heads = c_w_out.shape[1] // HEAD_DIM
    c_kv = (c_w_in.shape[2] - c_heads * HEAD_DIM) // (2 * HEAD_DIM)
    q_scale = HEAD_DIM ** -0.5 * LOG2E

    cond8 = jnp.zeros((8, d), F32).at[0].set(c_ctx).at[1].set(c[0])
    mod4 = _adaln(cond8, w_mod, b_mod).reshape(depth, 8, N_MOD, d)
    cos_t, sin_t = _rope_tables(seq, n_ctx)
    xa, h = _prenorm(x[0], ctx[0], norm_g, mod4)

    def col_vec(pieces):
        return jnp.concatenate([jnp.broadcast_to(jnp.asarray(v, F32), (n,)) for v, n in pieces]).reshape(1, -1)

    for i in range(depth):
        j = i // 2
        if i % 2 == 0:
            q0 = s5_width
            k0 = q0 + b_heads * HEAD_DIM
            v0 = k0 + b_kv * HEAD_DIM
            n_in = v0 + b_kv * HEAD_DIM
            proj = _inproj(h, ab_w_in[j].astype(BF16), cos_t, sin_t,
                           jnp.ones((1, n_in), F32),
                           col_vec([(1.0, q0), (q_scale, k0 - q0), (1.0, n_in - k0)]),
                           q0, v0, False)
            s5w = _s5_weights(s5_lam_re[j], s5_lam_im[j], s5_log_dt[j], s5_b_re[j], s5_b_im[j],
                              s5_c_re[j], s5_c_im[j], s5_d[j])
            a = _s5_glu(_s5(proj[:, :s5_width].astype(F32), n_ctx, s5w), glu_w[j].astype(BF16), glu_b[j])
            o = _window_attention(proj, attn_sink[j], n_ctx, q0, k0, v0, b_kv, b_heads // b_kv)
            w_out = ab_w_out[j].astype(BF16)
            y = _matmul([a, o], [w_out[:s5_width], w_out[s5_width:]], (1280, 640, 256), 512, BF16, 48, "ab_out")
        else:
            k0 = c_heads * HEAD_DIM
            v0 = k0 + c_kv * HEAD_DIM
            n_in = v0 + c_kv * HEAD_DIM
            proj = _inproj(h, c_w_in[j].astype(BF16), cos_t, sin_t,
                           col_vec([(jnp.tile(q_norm[j], c_heads), k0), (jnp.tile(k_norm[j], c_kv), v0 - k0),
                                    (1.0, n_in - v0)]),
                           col_vec([(q_scale, k0), (1.0, n_in - k0)]),
                           0, v0, True)
            o = _flash_attention(proj, n_ctx, 0, k0, v0, c_kv, c_heads // c_kv)
            y = _matmul([o], [c_w_out[j].astype(BF16)], (1280, 640, 256), 512, BF16, 48, "c_out")
        moe_layer = i % 2 == 1
        res = _resid(xa, y, norm_g, mod4, i, 1, 2, i, 2, 3, packed=moe_layer)
        xa, h = res[0], res[1]
        if not moe_layer:
            g = _swiglu_up(h, ffn_w1[j].astype(BF16), ffn_w3[j].astype(BF16))
            f = _matmul([g], [ffn_w2[j].astype(BF16)], (1280, 640, 256), 512, BF16, 56, "ffn_down")
            rows = None
        else:
            n_e = router_w.shape[2]
            src, wrow, blk_expert, n_used, rows = _moe_plan(_router(h, router_w[j]), n_e)
            f = _moe_experts(_moe_dispatch(res[2], src), wrow, blk_expert, n_used,
                             moe_w1[j].astype(BF16), moe_w3[j].astype(BF16), moe_w2[j].astype(BF16))
        if i + 1 < depth:
            xa, h = _resid(xa, f, norm_g, mod4, i, 3, 5, i + 1, 0, 0, rows=rows)
        else:
            out, = _resid(xa, f, norm_g, mod4, i, 3, 5, rows=rows, row_offset=n_ctx)
    return out[None]
```

```python
import functools
import math

import jax
import jax.numpy as jnp
from jax import lax
from jax.experimental import pallas as pl
from jax.experimental.pallas import tpu as pltpu

F32 = jnp.float32
BF16 = jnp.bfloat16

HEAD_DIM = 128
GRID_W = 64
ROPE_THETA = 10000.0
NORM_EPS = 1e-6
NEG = -1e30
N_MOD = 6
S5_GROUP = 16
S5_STATE = 64
S5_CHUNK = 16
WINDOW = 128
TOP_K = 2
LANES = 128
ROW_TILE = 256
MIB = 1024 * 1024


def _cp(sem, vmem_mib):
    return pltpu.CompilerParams(dimension_semantics=sem, vmem_limit_bytes=vmem_mib * MIB)


def _pick(n, cands):
    for t in cands:
        if n % t == 0:
            return t
    raise ValueError(f"no tile for {n} in {cands}")


def _rms(v):
    return v * lax.rsqrt(jnp.mean(v * v, axis=-1, keepdims=True) + NORM_EPS)


def _adaln_body(cond_ref, w_ref, b_ref, o_ref):
    c = cond_ref[...]
    a = (c * jax.nn.sigmoid(c)).astype(BF16)
    o_ref[...] = jnp.dot(a, w_ref[...].astype(BF16), preferred_element_type=F32) + b_ref[...]


def _adaln(cond8, w_mod, b_mod):
    depth, d, n6 = w_mod.shape
    tn = 512
    return pl.pallas_call(
        _adaln_body,
        grid=(depth, n6 // tn),
        in_specs=[pl.BlockSpec((8, d), lambda l, j: (0, 0)),
                  pl.BlockSpec((None, d, tn), lambda l, j: (l, 0, j)),
                  pl.BlockSpec((None, 1, tn), lambda l, j: (l, 0, j))],
        out_specs=pl.BlockSpec((None, 8, tn), lambda l, j: (l, 0, j)),
        out_shape=jax.ShapeDtypeStruct((depth, 8, n6), F32),
        compiler_params=_cp(("parallel", "parallel"), 40),
        name="adaln",
    )(cond8, w_mod, b_mod.reshape(depth, 1, n6))


def _prenorm_body(x_ref, ctx_ref, g_ref, mod_ref, xa_ref, h_ref):
    i = pl.program_id(0)

    def emit(v):
        xa_ref[...] = v
        n = _rms(v) * g_ref[0:1, :]
        h_ref[...] = (n * (1.0 + mod_ref[1:2, :]) + mod_ref[0:1, :]).astype(BF16)

    @pl.when(i == 0)
    def _():
        emit(ctx_ref[...])

    @pl.when(i > 0)
    def _():
        emit(x_ref[...])


def _prenorm(x2, ctx2, norm_g, mod4):
    seq, d = x2.shape
    nc = ctx2.shape[0]
    nt = seq + nc
    tr = ROW_TILE
    return pl.pallas_call(
        _prenorm_body,
        grid=(nt // tr,),
        in_specs=[pl.BlockSpec((tr, d), lambda i: (jnp.maximum(i - 1, 0), 0)),
                  pl.BlockSpec((tr, d), lambda i: (0, 0)),
                  pl.BlockSpec((None, 4, d), lambda i: (0, 0, 0)),
                  pl.BlockSpec((None, None, N_MOD, d), lambda i: (0, jnp.minimum(i, 1), 0, 0))],
        out_specs=[pl.BlockSpec((tr, d), lambda i: (i, 0)),
                   pl.BlockSpec((tr, d), lambda i: (i, 0))],
        out_shape=[jax.ShapeDtypeStruct((nt, d), F32), jax.ShapeDtypeStruct((nt, d), BF16)],
        compiler_params=_cp(("parallel",), 40),
        name="prenorm",
    )(x2, ctx2, norm_g, mod4)


def _row_gather_start(index_of, src_hbm, dst_ref, sem):
    def issue(r, carry):
        pltpu.make_async_copy(src_hbm.at[pl.ds(index_of(r), 1), :], dst_ref.at[pl.ds(r, 1), :], sem).start()
        return carry

    lax.fori_loop(0, dst_ref.shape[0], issue, 0, unroll=8)


def _row_gather_wait(src_hbm, dst_ref, sem):
    pltpu.make_async_copy(src_hbm.at[pl.ds(0, dst_ref.shape[0]), :], dst_ref, sem).wait()


def _pack_halves(v):
    half = v.shape[1] // 2
    vb = v.astype(BF16).astype(F32)
    hi = lax.bitcast_convert_type(vb[:, :half], jnp.uint32)
    lo = lax.bitcast_convert_type(vb[:, half:], jnp.uint32)
    return (hi & jnp.uint32(0xFFFF0000)) | (lo >> jnp.uint32(16))


def _unpack_halves(u):
    hi = lax.bitcast_convert_type(u & jnp.uint32(0xFFFF0000), F32)
    lo = lax.bitcast_convert_type(u << jnp.uint32(16), F32)
    return jnp.concatenate([hi.astype(BF16), lo.astype(BF16)], axis=1)


def _resid_body(*refs, post_idx, gate_idx, next_g_idx, next_shift_idx, gathered, packed, last):
    refs = list(refs)
    if gathered:
        p_ref = refs.pop(0)
    xa_ref, y_ref, gcur_ref, mcur_ref = refs[:4]
    refs = refs[4:]
    if not last:
        gnext_ref, mnext_ref = refs[:2]
        refs = refs[2:]
    xo_ref = refs.pop(0)
    if gathered:
        buf_ref, sem = refs[-2:]
        _row_gather_start(lambda r: p_ref[0, r], y_ref, buf_ref.at[0], sem)
        _row_gather_start(lambda r: p_ref[1, r], y_ref, buf_ref.at[1], sem)
        _row_gather_wait(y_ref, buf_ref.at[0], sem)
        _row_gather_wait(y_ref, buf_ref.at[1], sem)
        y = buf_ref[0] + buf_ref[1]
    else:
        y = y_ref[...].astype(F32)
    x = xa_ref[...] + mcur_ref[gate_idx:gate_idx + 1, :] * (_rms(y) * gcur_ref[post_idx:post_idx + 1, :])
    xo_ref[...] = x
    if last:
        return
    n = _rms(x) * gnext_ref[next_g_idx:next_g_idx + 1, :]
    h = (n * (1.0 + mnext_ref[next_shift_idx + 1:next_shift_idx + 2, :])
         + mnext_ref[next_shift_idx:next_shift_idx + 1, :])
    refs[0][...] = h.astype(BF16)
    if packed:
        refs[1][...] = _pack_halves(h)


def _resid(xa, y, norm_g, mod4, layer, post_idx, gate_idx, next_layer=None, next_g_idx=0, next_shift_idx=0,
           rows=None, packed=False, row_offset=0):
    nt, d = xa.shape
    tr = ROW_TILE
    off = row_offset // tr
    n_blk = (nt - row_offset) // tr
    last = next_layer is None
    gathered = rows is not None
    row = lambda i: (i + off, 0)
    mod_spec = lambda l: pl.BlockSpec((None, None, N_MOD, d), lambda i: (l, jnp.minimum(i + off, 1), 0, 0))
    in_specs, args = [], []
    if gathered:
        in_specs.append(pl.BlockSpec((None, 2, tr), lambda i: (i + off, 0, 0), memory_space=pltpu.SMEM))
        args.append(rows.reshape(2, nt // tr, tr).transpose(1, 0, 2))
    in_specs += [pl.BlockSpec((tr, d), row),
                 pl.BlockSpec(memory_space=pl.ANY) if gathered else pl.BlockSpec((tr, d), row),
                 pl.BlockSpec((None, 4, d), lambda i: (layer, 0, 0)), mod_spec(layer)]
    args += [xa, y, norm_g, mod4]
    out_specs = [pl.BlockSpec((tr, d), lambda i: (i, 0) if last else (i + off, 0))]
    out_shape = [jax.ShapeDtypeStruct((nt - row_offset if last else nt, d), F32)]
    if not last:
        in_specs += [pl.BlockSpec((None, 4, d), lambda i: (next_layer, 0, 0)), mod_spec(next_layer)]
        args += [norm_g, mod4]
        out_specs.append(pl.BlockSpec((tr, d), row))
        out_shape.append(jax.ShapeDtypeStruct((nt, d), BF16))
        if packed:
            out_specs.append(pl.BlockSpec((tr, d // 2), row))
            out_shape.append(jax.ShapeDtypeStruct((nt, d // 2), jnp.uint32))
    scratch = [pltpu.VMEM((2, tr, d), F32), pltpu.SemaphoreType.DMA(())] if gathered else []
    body = functools.partial(_resid_body, post_idx=post_idx, gate_idx=gate_idx, next_g_idx=next_g_idx,
                             next_shift_idx=next_shift_idx, gathered=gathered, packed=packed, last=last)
    return pl.pallas_call(
        body,
        grid=(n_blk,),
        in_specs=in_specs,
        out_specs=out_specs,
        out_shape=out_shape,
        scratch_shapes=scratch,
        input_output_aliases={} if last else {(1 if gathered else 0): 0},
        compiler_params=_cp(("arbitrary",) if gathered else ("parallel",), 48),
        name="resid",
    )(*args)


def _inproj_body(h_ref, w_ref, cos_ref, sin_ref, gain_ref, scale_ref, o_ref, *, rope_lo, rope_hi, use_norm):
    j = pl.program_id(1)
    acc = jnp.dot(h_ref[...], w_ref[...], preferred_element_type=F32)
    is_rope = jnp.logical_and(j >= rope_lo, j < rope_hi)

    @pl.when(jnp.logical_not(is_rope))
    def _():
        o_ref[...] = acc.astype(BF16)

    @pl.when(is_rope)
    def _():
        c = cos_ref[...]
        s = sin_ref[...]
        lane = lax.broadcasted_iota(jnp.int32, c.shape, 1)
        first = (lane & 32) == 0
        for hh in range(acc.shape[1] // HEAD_DIM):
            sl = slice(hh * HEAD_DIM, (hh + 1) * HEAD_DIM)
            blk = acc[:, sl]
            if use_norm:
                blk = _rms(blk) * gain_ref[:, sl]
            rot = jnp.where(first, pltpu.roll(blk, HEAD_DIM - 32, 1), pltpu.roll(blk, 32, 1))
            o_ref[:, sl] = ((blk * c + rot * s) * scale_ref[:, sl]).astype(BF16)


def _inproj(h, w, layer, cos_t, sin_t, gain, scale, rope_lo, rope_hi, use_norm):
    nt, d = h.shape
    n = w.shape[2]
    tm = _pick(nt, (1280, 640, 256))
    tn = 512
    body = functools.partial(_inproj_body, rope_lo=rope_lo // tn, rope_hi=rope_hi // tn, use_norm=use_norm)
    return pl.pallas_call(
        body,
        grid=(nt // tm, n // tn),
        in_specs=[pl.BlockSpec((tm, d), lambda i, j: (i, 0)),
                  pl.BlockSpec((None, d, tn), lambda i, j: (layer, 0, j)),
                  pl.BlockSpec((tm, HEAD_DIM), lambda i, j: (i, 0)),
                  pl.BlockSpec((tm, HEAD_DIM), lambda i, j: (i, 0)),
                  pl.BlockSpec((1, tn), lambda i, j: (0, j)),
                  pl.BlockSpec((1, tn), lambda i, j: (0, j))],
        out_specs=pl.BlockSpec((tm, tn), lambda i, j: (i, j)),
        out_shape=jax.ShapeDtypeStruct((nt, n), BF16),
        compiler_params=_cp(("parallel", "arbitrary"), 48),
        name="inproj",
    )(h, w, cos_t, sin_t, gain, scale)


def _mm_body(*refs, n_a):
    o_ref = refs[2 * n_a]
    acc = jnp.dot(refs[0][...], refs[n_a][...], preferred_element_type=F32)
    for t in range(1, n_a):
        acc += jnp.dot(refs[t][...], refs[n_a + t][...], preferred_element_type=F32)
    o_ref[...] = acc.astype(o_ref.dtype)


def _matmul(a_blocks, w, layer, kw, tm_cands, tn, out_dtype, vmem_mib, name):
    nt = a_blocks[0][0].shape[0]
    n = w.shape[2]
    tm = _pick(nt, tm_cands)
    in_specs = [pl.BlockSpec((tm, kw), lambda i, j, c=c: (i, c)) for _, c in a_blocks]
    in_specs += [pl.BlockSpec((None, kw, tn), lambda i, j, t=t: (layer, t, j)) for t in range(len(a_blocks))]
    return pl.pallas_call(
        functools.partial(_mm_body, n_a=len(a_blocks)),
        grid=(nt // tm, n // tn),
        in_specs=in_specs,
        out_specs=pl.BlockSpec((tm, tn), lambda i, j: (i, j)),
        out_shape=jax.ShapeDtypeStruct((nt, n), out_dtype),
        compiler_params=_cp(("parallel", "arbitrary"), vmem_mib),
        name=name,
    )(*[a for a, _ in a_blocks], *([w] * len(a_blocks)))


def _swiglu_up_body(h_ref, w1_ref, w3_ref, o_ref):
    h = h_ref[...]
    a = jnp.dot(h, w1_ref[...], preferred_element_type=F32)
    b = jnp.dot(h, w3_ref[...], preferred_element_type=F32)
    o_ref[...] = (a * jax.nn.sigmoid(a) * b).astype(BF16)


def _swiglu_up(h, w1, w3, layer):
    nt, d = h.shape
    f = w1.shape[2]
    tm, tn = _pick(nt, (1280, 640, 256)), 512
    w_spec = pl.BlockSpec((None, d, tn), lambda i, j: (layer, 0, j))
    return pl.pallas_call(
        _swiglu_up_body,
        grid=(nt // tm, f // tn),
        in_specs=[pl.BlockSpec((tm, d), lambda i, j: (i, 0)), w_spec, w_spec],
        out_specs=pl.BlockSpec((tm, tn), lambda i, j: (i, j)),
        out_shape=jax.ShapeDtypeStruct((nt, f), BF16),
        compiler_params=_cp(("parallel", "arbitrary"), 52),
        name="swiglu_up",
    )(h, w1, w3)


def _router_body(h_ref, whi_ref, wlo_ref, g_ref, *, n_experts):
    h = h_ref[...]
    logits = (jnp.dot(h, whi_ref[...], preferred_element_type=F32)
              + jnp.dot(h, wlo_ref[...], preferred_element_type=F32))
    lane = lax.broadcasted_iota(jnp.int32, logits.shape, 1)
    lg = jnp.where(lane < n_experts, logits, NEG)
    v1 = jnp.max(lg, axis=-1, keepdims=True)
    i1 = jnp.min(jnp.where(lg == v1, lane, LANES), axis=-1, keepdims=True)
    lg2 = jnp.where(lane == i1, NEG, lg)
    v2 = jnp.max(lg2, axis=-1, keepdims=True)
    i2 = jnp.min(jnp.where(lg2 == v2, lane, LANES), axis=-1, keepdims=True)
    e = jnp.exp(v2 - v1)
    den = 1.0 + e
    g_ref[...] = (jnp.where(lane == i1, 1.0 / den, 0.0) + jnp.where(lane == i2, e / den, 0.0)
                  + jnp.where(lane == n_experts, i1.astype(F32), 0.0)
                  + jnp.where(lane == n_experts + 1, i2.astype(F32), 0.0))


def _router(h, router_w):
    nt, d = h.shape
    n_e = router_w.shape[1]
    whi = router_w.astype(BF16)
    wlo = (router_w - whi.astype(F32)).astype(BF16)
    pad = ((0, 0), (0, LANES - n_e))
    whi = jnp.pad(whi, pad)
    wlo = jnp.pad(wlo, pad)
    tm = _pick(nt, (640, 256))
    return pl.pallas_call(
        functools.partial(_router_body, n_experts=n_e),
        grid=(nt // tm,),
        in_specs=[pl.BlockSpec((tm, d), lambda i: (i, 0)),
                  pl.BlockSpec((d, LANES), lambda i: (0, 0)),
                  pl.BlockSpec((d, LANES), lambda i: (0, 0))],
        out_specs=pl.BlockSpec((tm, LANES), lambda i: (i, 0)),
        out_shape=jax.ShapeDtypeStruct((nt, LANES), F32),
        compiler_params=_cp(("parallel",), 32),
        name="router",
    )(h, whi, wlo)


MOE_ROWS = 512


def _moe_plan(gates, n_e):
    nt = gates.shape[0]
    tg = MOE_ROWS
    i1 = gates[:, n_e].astype(jnp.int32)
    i2 = gates[:, n_e + 1].astype(jnp.int32)
    w1 = jnp.take_along_axis(gates, i1[:, None], axis=1)[:, 0]
    w2 = jnp.take_along_axis(gates, i2[:, None], axis=1)[:, 0]
    experts = jnp.arange(n_e, dtype=jnp.int32)
    chosen = ((i1[:, None] == experts) | (i2[:, None] == experts)).astype(jnp.int32)
    rank = jnp.cumsum(chosen, axis=0) - chosen
    padded = (jnp.sum(chosen, axis=0) + tg - 1) // tg * tg
    ends = jnp.cumsum(padded)
    pos = (ends - padded)[None, :] + rank
    p1 = jnp.take_along_axis(pos, i1[:, None], axis=1)[:, 0]
    p2 = jnp.take_along_axis(pos, i2[:, None], axis=1)[:, 0]
    n_rows = 2 * nt + n_e * tg
    token = jnp.arange(nt, dtype=jnp.int32)
    p12 = jnp.concatenate([p1, p2])
    src = jnp.zeros((n_rows,), jnp.int32).at[p12].set(jnp.concatenate([token, token]))
    wrow = jnp.zeros((n_rows,), F32).at[p12].set(jnp.concatenate([w1, w2]))
    blk_start = jnp.arange(n_rows // tg, dtype=jnp.int32) * tg
    blk_expert = jnp.minimum(jnp.searchsorted(ends, blk_start, side='right'), n_e - 1).astype(jnp.int32)
    n_used = (ends[-1] // tg).astype(jnp.int32).reshape(1)
    return src, wrow, blk_expert, n_used, jnp.stack([p1, p2]).astype(jnp.int32)


def _dispatch_body(src_ref, hp_hbm, o_ref, sem):
    _row_gather_start(lambda r: src_ref[0, r], hp_hbm, o_ref, sem)
    _row_gather_wait(hp_hbm, o_ref, sem)


def _moe_dispatch(hp, src):
    n_rows, tg = src.shape[0], MOE_ROWS
    return pl.pallas_call(
        _dispatch_body,
        grid=(n_rows // tg,),
        in_specs=[pl.BlockSpec((None, 1, tg), lambda b: (b, 0, 0), memory_space=pltpu.SMEM),
                  pl.BlockSpec(memory_space=pl.ANY)],
        out_specs=pl.BlockSpec((tg, hp.shape[1]), lambda b: (b, 0)),
        out_shape=jax.ShapeDtypeStruct((n_rows, hp.shape[1]), jnp.uint32),
        scratch_shapes=[pltpu.SemaphoreType.DMA(())],
        compiler_params=_cp(("arbitrary",), 32),
        name="moe_dispatch",
    )(src.reshape(n_rows // tg, 1, tg), hp)


def _moe_up_body(expert_ref, used_ref, x_ref, w1_ref, w3_ref, gate_ref, o_ref):
    b = pl.program_id(0)

    @pl.when(b < used_ref[0])
    def _():
        x = _unpack_halves(x_ref[...])
        a = jnp.dot(x, w1_ref[...], preferred_element_type=F32)
        c = jnp.dot(x, w3_ref[...], preferred_element_type=F32)
        o_ref[...] = (a * jax.nn.sigmoid(a) * c * gate_ref[...]).astype(BF16)

    @pl.when(b >= used_ref[0])
    def _():
        o_ref[...] = jnp.zeros(o_ref.shape, o_ref.dtype)


def _moe_down_body(expert_ref, used_ref, g_ref, w2_ref, o_ref):
    b = pl.program_id(0)

    @pl.when(b < used_ref[0])
    def _():
        o_ref[...] = jnp.dot(g_ref[...], w2_ref[...], preferred_element_type=F32)

    @pl.when(b >= used_ref[0])
    def _():
        o_ref[...] = jnp.zeros(o_ref.shape, o_ref.dtype)


def _moe_experts(xs, wrow, blk_expert, n_used, w1, w3, w2, layer):
    n_rows, tg = xs.shape[0], MOE_ROWS
    _, n_e, d, f = w1.shape
    row = lambda b, e, u: (b, 0)
    per_expert = lambda b, e, u: (layer, e[b], 0, 0)
    g = pl.pallas_call(
        _moe_up_body,
        grid_spec=pltpu.PrefetchScalarGridSpec(
            num_scalar_prefetch=2, grid=(n_rows // tg,),
            in_specs=[pl.BlockSpec((tg, xs.shape[1]), row), pl.BlockSpec((None, None, d, f), per_expert),
                      pl.BlockSpec((None, None, d, f), per_expert), pl.BlockSpec((tg, 1), row)],
            out_specs=pl.BlockSpec((tg, f), row)),
        out_shape=jax.ShapeDtypeStruct((n_rows, f), BF16),
        compiler_params=_cp(("arbitrary",), 56),
        name="moe_up",
    )(blk_expert, n_used, xs, w1, w3, wrow.reshape(n_rows, 1))
    return pl.pallas_call(
        _moe_down_body,
        grid_spec=pltpu.PrefetchScalarGridSpec(
            num_scalar_prefetch=2, grid=(n_rows // tg,),
            in_specs=[pl.BlockSpec((tg, f), row), pl.BlockSpec((None, None, f, d), per_expert)],
            out_specs=pl.BlockSpec((tg, d), row)),
        out_shape=jax.ShapeDtypeStruct((n_rows, d), F32),
        compiler_params=_cp(("arbitrary",), 48),
        name="moe_down",
    )(blk_expert, n_used, g, w2)


V_ROWS = HEAD_DIM + 16
LOG2E = math.log2(math.e)


def _stack_heads(q, n_grp):
    return jnp.concatenate([q[:, g * HEAD_DIM:(g + 1) * HEAD_DIM] for g in range(n_grp)], axis=0)


def _unstack_store(o_ref, o, n_grp, rows):
    for g in range(n_grp):
        o_ref[:, g * HEAD_DIM:(g + 1) * HEAD_DIM] = o[g * rows:(g + 1) * rows, :].astype(o_ref.dtype)


def _scores_t(k, q):
    return lax.dot_general(k, q, (((1,), (1,)), ((), ())), preferred_element_type=F32)


def _transpose_v(v, n_kv):
    nt = v.shape[0]
    vt = v.reshape(nt, n_kv, HEAD_DIM).transpose(1, 2, 0)
    return jnp.concatenate([vt, jnp.ones((n_kv, V_ROWS - HEAD_DIM, nt), v.dtype)], axis=1)


def _win_body(sink_ref, q_ref, km_ref, k0_ref, kp_ref, kc_ref, vm_ref, v0_ref, vp_ref, vc_ref, o_ref, *,
              n_grp, ctx_blocks, n_lat):
    kvh = pl.program_id(0)
    n = pl.program_id(1)
    blk = q_ref.shape[0]
    qs = _stack_heads(q_ref[...], n_grp)
    keys = jnp.concatenate([km_ref[...], k0_ref[...], kp_ref[...], kc_ref[...]], axis=0)
    vt = jnp.concatenate([vm_ref[...], v0_ref[...], vp_ref[...], vc_ref[...]], axis=1)
    st = _scores_t(keys, qs)
    n_win = 3 * blk
    k_off = lax.broadcasted_iota(jnp.int32, st.shape, 0) - blk
    q_off = lax.broadcasted_iota(jnp.int32, st.shape, 1) % blk
    kpos = (n - ctx_blocks) * blk + k_off
    in_band = ((jnp.abs(q_off - k_off) <= WINDOW) & (kpos >= 0) & (kpos < n_lat) & (n >= ctx_blocks))
    st = jnp.where(in_band | (k_off >= n_win - blk), st, NEG)
    sink = jnp.concatenate(
        [jnp.full((1, blk), sink_ref[kvh * n_grp + g] * LOG2E, F32) for g in range(n_grp)], axis=1)
    m = jnp.maximum(jnp.max(st, axis=0, keepdims=True), sink)
    pt = jnp.exp2(st - m).astype(BF16)
    acc = jnp.dot(vt, pt, preferred_element_type=F32)
    den = acc[HEAD_DIM:HEAD_DIM + 1, :] + jnp.exp2(sink - m)
    _unstack_store(o_ref, (acc[0:HEAD_DIM, :] / den).T, n_grp, blk)


def _window_attention(proj, sink, n_ctx, q_col, k_col, v_col, n_kv, n_grp):
    nt = proj.shape[0]
    blk = WINDOW
    nb = nt // blk
    cb = n_ctx // blk
    qc, kc = q_col // (n_grp * HEAD_DIM), k_col // HEAD_DIM
    vt = _transpose_v(proj[:, v_col:v_col + n_kv * HEAD_DIM], n_kv)
    lo = lambda n: jnp.maximum(n - 1, cb)
    hi = lambda n: jnp.minimum(n + 1, nb - 1)
    same = lambda n: n
    k_tile = lambda rowf: pl.BlockSpec((blk, HEAD_DIM), lambda h, n: (rowf(n), kc + h))
    v_tile = lambda rowf: pl.BlockSpec((None, V_ROWS, blk), lambda h, n: (h, 0, rowf(n)))
    body = functools.partial(_win_body, n_grp=n_grp, ctx_blocks=cb, n_lat=nt - n_ctx)
    return pl.pallas_call(
        body,
        grid=(n_kv, nb),
        in_specs=[pl.BlockSpec(memory_space=pltpu.SMEM),
                  pl.BlockSpec((blk, n_grp * HEAD_DIM), lambda h, n: (n, qc + h)),
                  k_tile(lo), k_tile(same), k_tile(hi),
                  pl.BlockSpec((n_ctx, HEAD_DIM), lambda h, n: (0, kc + h)),
                  v_tile(lo), v_tile(same), v_tile(hi),
                  pl.BlockSpec((None, V_ROWS, n_ctx), lambda h, n: (h, 0, 0))],
        out_specs=pl.BlockSpec((blk, n_grp * HEAD_DIM), lambda h, n: (n, h)),
        out_shape=jax.ShapeDtypeStruct((nt, n_kv * n_grp * HEAD_DIM), BF16),
        compiler_params=_cp(("parallel", "parallel"), 32),
        name="window_attn",
    )(sink, proj, proj, proj, proj, proj, vt, vt, vt, vt)


def _flash_body(q_ref, k_ref, vt_ref, o_ref, s_ref, acc_ref, *, n_grp, tk, n_chunks, n_ctx):
    qi = pl.program_id(1)
    tq = q_ref.shape[0]
    qs = _stack_heads(q_ref[...], n_grp)
    is_ctx = qi == 0

    def scores(c):
        start = c * tk if isinstance(c, int) else pl.multiple_of(c * tk, tk)
        return _scores_t(k_ref[pl.ds(start, tk), :], qs)

    def absorb(c, slot, m_old):
        m_new = jnp.maximum(m_old, jnp.max(s_ref[slot], axis=0, keepdims=True))
        alpha = jnp.exp2(m_old - m_new)
        pt = jnp.exp2(s_ref[slot] - m_new).astype(BF16)
        acc_ref[...] = alpha * acc_ref[...] + jnp.dot(vt_ref[c], pt, preferred_element_type=F32)
        return m_new

    acc_ref[...] = jnp.zeros(acc_ref.shape, F32)
    st0 = scores(0)
    row = lax.broadcasted_iota(jnp.int32, st0.shape, 0)
    s_ref[0] = jnp.where(row < jnp.where(is_ctx, n_ctx, tk), st0, NEG)
    m = jnp.full((1, qs.shape[0]), NEG, F32)

    def pair(j, m):
        c = 2 * j
        s_ref[1] = scores(c + 1)
        m = absorb(c, 0, m)
        s_ref[0] = scores(c + 2)
        return absorb(c + 1, 1, m)

    n_pairs = (n_chunks - 1) // 2
    m = lax.fori_loop(0, jnp.where(is_ctx, 0, n_pairs), pair, m)
    if (n_chunks - 1) % 2 == 0:
        absorb(jnp.where(is_ctx, 0, n_chunks - 1), 0, m)
    else:
        s_ref[1] = jnp.where(is_ctx, NEG, scores(n_chunks - 1))
        m = absorb(jnp.where(is_ctx, 0, n_chunks - 2), 0, m)
        absorb(n_chunks - 1, 1, m)
    acc = acc_ref[...]
    _unstack_store(o_ref, (acc[0:HEAD_DIM, :] / acc[HEAD_DIM:HEAD_DIM + 1, :]).T, n_grp, tq)


def _flash_attention(proj, n_ctx, q_col, k_col, v_col, n_kv, n_grp):
    nt = proj.shape[0]
    tq = n_ctx
    tk = _pick(nt, (1280, 256))
    n_chunks = nt // tk
    qc, kc = q_col // (n_grp * HEAD_DIM), k_col // HEAD_DIM
    vt = _transpose_v(proj[:, v_col:v_col + n_kv * HEAD_DIM], n_kv)
    vt = vt.reshape(n_kv, V_ROWS, n_chunks, tk).transpose(0, 2, 1, 3)
    body = functools.partial(_flash_body, n_grp=n_grp, tk=tk, n_chunks=n_chunks, n_ctx=n_ctx)
    return pl.pallas_call(
        body,
        grid=(n_kv, nt // tq),
        in_specs=[pl.BlockSpec((tq, n_grp * HEAD_DIM), lambda h, i: (i, qc + h)),
                  pl.BlockSpec((nt, HEAD_DIM), lambda h, i: (0, kc + h)),
                  pl.BlockSpec((None, n_chunks, V_ROWS, tk), lambda h, i: (h, 0, 0, 0))],
        out_specs=pl.BlockSpec((tq, n_grp * HEAD_DIM), lambda h, i: (i, h)),
        out_shape=jax.ShapeDtypeStruct((nt, n_kv * n_grp * HEAD_DIM), BF16),
        scratch_shapes=[pltpu.VMEM((2, tk, n_grp * tq), F32), pltpu.VMEM((V_ROWS, n_grp * tq), F32)],
        compiler_params=_cp(("parallel", "parallel"), 56),
        name="flash_attn",
    )(proj, proj, vt)


def _s5_weights(lam_re, lam_im, log_dt, b_re, b_im, c_re, c_im, d_skip):
    hp = lax.Precision.HIGHEST
    n_g, n_p = lam_re.shape[1], lam_re.shape[2]
    n_h = b_re.shape[-1]
    ell = S5_CHUNK
    dt = jnp.exp(log_dt)[..., None]
    ar, ai = lam_re * dt, lam_im * dt
    k = jnp.arange(ell + 1, dtype=F32)[:, None, None, None]
    mag = jnp.exp(k * ar[None])
    lp_re, lp_im = mag * jnp.cos(k * ai[None]), mag * jnp.sin(k * ai[None])
    nr, ni = lp_re[1] - 1.0, lp_im[1]
    den = lam_re * lam_re + lam_im * lam_im
    fr, fi = (nr * lam_re + ni * lam_im) / den, (ni * lam_re - nr * lam_im) / den
    bb_re = fr[..., None] * b_re - fi[..., None] * b_im
    bb_im = fr[..., None] * b_im + fi[..., None] * b_re
    lpe_re, lpe_im = lp_re[:, :, :, None, :], lp_im[:, :, :, None, :]
    cl_re = c_re[None] * lpe_re - c_im[None] * lpe_im
    cl_im = c_re[None] * lpe_im + c_im[None] * lpe_re
    m = (jnp.einsum('kdghp,dgpi->kdghi', cl_re, bb_re, precision=hp)
         - jnp.einsum('kdghp,dgpi->kdghi', cl_im, bb_im, precision=hp))
    gl = LANES // n_h
    n_q = n_g // gl
    lane = jnp.arange(LANES)
    put_h = (lane[None, None, :] == (jnp.arange(gl)[:, None, None] * n_h
                                     + jnp.arange(n_h)[None, :, None])).astype(F32)
    put_p = (lane[None, None, :] == (jnp.arange(2)[:, None, None] * n_p
                                     + jnp.arange(n_p)[None, :, None])).astype(F32)
    m_lag = jnp.concatenate([m[ell - 1:0:-1, 1], (m[0, 0] + m[0, 1])[None], m[1:ell, 0]], axis=0)
    m_lag = m_lag.reshape(2 * ell - 1, n_q, gl, n_h, n_h)
    toep_c = jnp.einsum('dqghi,gir,ghl->dqrl', m_lag, put_h, put_h, precision=hp)

    def state_in(direction, powers):
        pr = lp_re[powers, direction].transpose(1, 0, 2)[:, :, None, :]
        pi = lp_im[powers, direction].transpose(1, 0, 2)[:, :, None, :]
        br = bb_re[direction].transpose(0, 2, 1)[:, None, :, :]
        bi = bb_im[direction].transpose(0, 2, 1)[:, None, :, :]
        return pr * br - pi * bi, pr * bi + pi * br

    steps = jnp.arange(ell)
    f_re, f_im = state_in(0, ell - 1 - steps)
    g_re, g_im = state_in(1, steps)
    w_in = jnp.stack([f_re, f_im, g_re, g_im], axis=3)
    w_in = w_in.reshape(n_q, gl // 2, 2, ell, n_h, 4, n_p)
    w_in_c = jnp.einsum('qrsaikp,spl->qarskil', w_in, put_p, precision=hp)
    w_in_c = w_in_c.reshape(n_q, ell, gl, 4, n_h, LANES)

    def state_out(direction, powers):
        wr = cl_re[powers, direction].transpose(1, 3, 0, 2)
        wi = -cl_im[powers, direction].transpose(1, 3, 0, 2)
        w = jnp.stack([wr, wi], axis=1).reshape(n_q, gl, 2, n_p, ell, n_h)
        return jnp.einsum('qgepth,ghl->qgetpl', w, put_h, precision=hp)

    w_of_c = state_out(0, steps + 1)
    w_ob_c = state_out(1, ell - steps)
    lam_l = jnp.stack([lp_re[ell, 0], lp_im[ell, 0], lp_re[ell, 1], lp_im[ell, 1]], axis=0)
    lam_l = lam_l.reshape(4, n_g // 2, 2 * n_p)
    d_q = jnp.broadcast_to(d_skip.reshape(n_q, 1, gl, n_h), (n_q, ell, gl, n_h)).reshape(n_q, 1, ell * LANES)
    return toep_c.astype(BF16), w_in_c.astype(BF16), w_of_c.astype(BF16), w_ob_c.astype(BF16), lam_l, d_q


def _s5_build_toeplitz(toep_ref, t_ref):
    ell = S5_CHUNK
    for a in range(ell):
        for b in range(ell):
            t_ref[a * LANES:(a + 1) * LANES, b * LANES:(b + 1) * LANES] = toep_ref[b - a + ell - 1]


def _s5_build_state_in(w_ref, t_ref):
    _, gl, kinds, n_h, _ = w_ref.shape
    t_ref[...] = jnp.zeros(t_ref.shape, t_ref.dtype)
    for a in range(S5_CHUNK):
        for g in range(gl):
            r0 = (a * gl + g) * n_h
            for k in range(kinds):
                c0 = ((g // 2) * kinds + k) * LANES
                t_ref[r0:r0 + n_h, c0:c0 + LANES] = w_ref[a, g, k]


def _s5_build_state_out(w_ref, t_ref):
    gl, _, _, n_p, _ = w_ref.shape
    for g in range(gl):
        for e in range(2):
            r0 = ((g // 2) * 4 + e * 2 + g % 2) * n_p
            for t in range(S5_CHUNK):
                t_ref[r0:r0 + n_p, t * LANES:(t + 1) * LANES] = w_ref[g, e, t]


def _s5_chunk_rows(u_ref, n_rows):
    return jnp.concatenate([u_ref[pl.ds(t, n_rows, stride=S5_CHUNK), :] for t in range(S5_CHUNK)], axis=1)


def _s5_in_body(u_ref, w_ref, x_ref, win_ref):
    @pl.when(pl.program_id(1) == 0)
    def _():
        _s5_build_state_in(w_ref, win_ref)

    uc = _s5_chunk_rows(u_ref, x_ref.shape[0]).astype(BF16)
    x_ref[...] = jnp.dot(uc, win_ref[...], preferred_element_type=F32)


def _s5_scan_body(xf_ref, xb_ref, lam_ref, sf_ref, sb_ref, st_ref):
    @pl.when(pl.program_id(0) == 0)
    def _():
        st_ref[...] = jnp.zeros(st_ref.shape, F32)

    half = xf_ref.shape[2] // 2
    rows = xf_ref.shape[0]

    def run(x_ref, s_ref, base, order):
        ar, ai = lam_ref[base], lam_ref[base + 1]
        sr, si = st_ref[base], st_ref[base + 1]
        for r in order:
            s_ref[r, :, 0:half] = sr
            s_ref[r, :, half:2 * half] = si
            xr, xi = x_ref[r, :, 0:half], x_ref[r, :, half:2 * half]
            sr, si = ar * sr - ai * si + xr, ar * si + ai * sr + xi
        st_ref[base] = sr
        st_ref[base + 1] = si

    run(xf_ref, sf_ref, 0, range(rows))
    run(xb_ref, sb_ref, 2, range(rows - 1, -1, -1))


def _s5_out_body(u_ref, sf_ref, sb_ref, toep_ref, wof_ref, wob_ref, d_ref, y_ref, t_ref, of_ref, ob_ref):
    @pl.when(pl.program_id(1) == 0)
    def _():
        _s5_build_toeplitz(toep_ref, t_ref)
        _s5_build_state_out(wof_ref, of_ref)
        _s5_build_state_out(wob_ref, ob_ref)

    n_rows = sf_ref.shape[0]
    uc = _s5_chunk_rows(u_ref, n_rows)
    y = jnp.dot(uc.astype(BF16), t_ref[...], preferred_element_type=F32)
    y += jnp.dot(sf_ref[...].astype(BF16), of_ref[...], preferred_element_type=F32)
    y += jnp.dot(sb_ref[...].astype(BF16), ob_ref[...], preferred_element_type=F32)
    y += uc * d_ref[...]
    for t in range(S5_CHUNK):
        y_ref[pl.ds(t, n_rows, stride=S5_CHUNK), :] = y[:, t * LANES:(t + 1) * LANES]


def _s5(u, n_ctx, weights):
    toep_c, w_in_c, w_of_c, w_ob_c, lam_l, d_q = weights
    nt, width = u.shape
    ell = S5_CHUNK
    n_q, _, gl, kinds, _, _ = w_in_c.shape
    kdim = ell * LANES
    n_pair = lam_l.shape[1]
    sdim = kinds * LANES
    sdim_q = (gl // 2) * sdim
    n_chunk = nt // ell
    cb = _pick(n_chunk, (208, 240, 80))
    u_spec = pl.BlockSpec((cb * ell, LANES), lambda q, i: (i, q))

    def per_q(a):
        return pl.BlockSpec((None,) + a.shape[1:], lambda q, i: (q,) + (0,) * (a.ndim - 1))

    x = pl.pallas_call(
        _s5_in_body,
        grid=(n_q, n_chunk // cb),
        in_specs=[u_spec, per_q(w_in_c)],
        out_specs=pl.BlockSpec((cb, sdim_q), lambda q, i: (i, q)),
        out_shape=jax.ShapeDtypeStruct((n_chunk, n_q * sdim_q), F32),
        scratch_shapes=[pltpu.VMEM((kdim, sdim_q), BF16)],
        compiler_params=_cp(("arbitrary", "arbitrary"), 40),
        name="s5_state_in",
    )(u, w_in_c)
    xt = x.reshape(n_chunk, n_pair, sdim)
    rb = n_ctx // ell
    nb = n_chunk // rb
    half = sdim // 2
    bwd = lambda b: (jnp.where(b == 0, 0, nb - b), 0, 0)
    sf, sb = pl.pallas_call(
        _s5_scan_body,
        grid=(nb,),
        in_specs=[pl.BlockSpec((rb, n_pair, half), lambda b: (b, 0, 0)),
                  pl.BlockSpec((rb, n_pair, half), lambda b: (jnp.where(b == 0, 0, nb - b), 0, 1)),
                  pl.BlockSpec((4, n_pair, half // 2), lambda b: (0, 0, 0))],
        out_specs=[pl.BlockSpec((rb, n_pair, half), lambda b: (b, 0, 0)),
                   pl.BlockSpec((rb, n_pair, half), bwd)],
        out_shape=[jax.ShapeDtypeStruct((n_chunk, n_pair, half), F32),
                   jax.ShapeDtypeStruct((n_chunk, n_pair, half), F32)],
        scratch_shapes=[pltpu.VMEM((4, n_pair, half // 2), F32)],
        compiler_params=_cp(("arbitrary",), 32),
        name="s5_scan",
    )(xt, xt, lam_l)
    sdim_h = sdim_q // 2
    s_spec = pl.BlockSpec((cb, sdim_h), lambda q, i: (i, q))
    toep_spec = pl.BlockSpec((toep_c.shape[0], None, LANES, LANES), lambda q, i: (0, q, 0, 0))
    return pl.pallas_call(
        _s5_out_body,
        grid=(n_q, n_chunk // cb),
        in_specs=[u_spec, s_spec, s_spec, toep_spec, per_q(w_of_c), per_q(w_ob_c), per_q(d_q)],
        out_specs=u_spec,
        out_shape=jax.ShapeDtypeStruct((nt, width), F32),
        scratch_shapes=[pltpu.VMEM((kdim, kdim), BF16), pltpu.VMEM((sdim_h, kdim), BF16),
                        pltpu.VMEM((sdim_h, kdim), BF16)],
        compiler_params=_cp(("arbitrary", "arbitrary"), 56),
        name="s5_out",
    )(u, sf.reshape(n_chunk, n_pair * half), sb.reshape(n_chunk, n_pair * half), toep_c, w_of_c, w_ob_c, d_q)


def _glu_body(y_ref, w_ref, b_ref, o_ref):
    y = y_ref[...]
    gy = 0.5 * y * (1.0 + jnp.tanh(math.sqrt(2.0 / math.pi) * (y + 0.044715 * (y * y * y))))
    z = jnp.dot(gy.astype(BF16), w_ref[...], preferred_element_type=F32) + b_ref[...]
    o_ref[...] = (gy * jax.nn.sigmoid(z)).astype(BF16)


def _s5_glu(y, glu_w, glu_b):
    nt, width = y.shape
    tm = _pick(nt, (1280, 640, 256))
    return pl.pallas_call(
        _glu_body,
        grid=(nt // tm,),
        in_specs=[pl.BlockSpec((tm, width), lambda i: (i, 0)),
                  pl.BlockSpec((width, width), lambda i: (0, 0)),
                  pl.BlockSpec((1, width), lambda i: (0, 0))],
        out_specs=pl.BlockSpec((tm, width), lambda i: (i, 0)),
        out_shape=jax.ShapeDtypeStruct((nt, width), BF16),
        compiler_params=_cp(("parallel",), 40),
        name="s5_glu",
    )(y, glu_w, glu_b.reshape(1, width))


def _rope_tables(seq, n_ctx):
    t = jnp.arange(seq)
    row_id = (t // GRID_W).astype(F32)
    col_id = (t % GRID_W).astype(F32)
    n_freq = HEAD_DIM // 4
    inv = ROPE_THETA ** (-jnp.arange(n_freq, dtype=F32) / n_freq)
    ang_r = row_id[:, None] * inv[None, :]
    ang_c = col_id[:, None] * inv[None, :]
    cos_t = jnp.concatenate([jnp.cos(ang_r), jnp.cos(ang_r), jnp.cos(ang_c), jnp.cos(ang_c)], axis=1)
    sin_t = jnp.concatenate([-jnp.sin(ang_r), jnp.sin(ang_r), -jnp.sin(ang_c), jnp.sin(ang_c)], axis=1)
    cos_t = jnp.concatenate([jnp.ones((n_ctx, HEAD_DIM), F32), cos_t], axis=0)
    sin_t = jnp.concatenate([jnp.zeros((n_ctx, HEAD_DIM), F32), sin_t], axis=0)
    return cos_t, sin_t


def kernel(x, c, ctx, c_ctx, w_mod, b_mod, norm_g, ab_w_in, ab_w_out, s5_lam_re, s5_lam_im, s5_log_dt, s5_b_re, s5_b_im, s5_c_re, s5_c_im, s5_d, glu_w, glu_b, attn_sink, ffn_w1, ffn_w3, ffn_w2, c_w_in, c_w_out, q_norm, k_norm, router_w, moe_w1, moe_w3, moe_w2):
    assert x.shape[0] == 1 and ctx.shape[0] == 1
    seq, d = x.shape[1], x.shape[2]
    n_ctx = ctx.shape[1]
    assert n_ctx == ROW_TILE
    depth = w_mod.shape[0]
    s5_width = s5_d.shape[1] * s5_d.shape[2]
    b_heads = attn_sink.shape[1]
    b_kv = (ab_w_in.shape[2] - s5_width - b_heads * HEAD_DIM) // (2 * HEAD_DIM)
    c_heads = c_w_out.shape[1] // HEAD_DIM
    c_kv = (c_w_in.shape[2] - c_heads * HEAD_DIM) // (2 * HEAD_DIM)
    q_scale = HEAD_DIM ** -0.5 * LOG2E

    cond8 = jnp.zeros((8, d), F32).at[0].set(c_ctx).at[1].set(c[0])
    mod4 = _adaln(cond8, w_mod, b_mod).reshape(depth, 8, N_MOD, d)
    cos_t, sin_t = _rope_tables(seq, n_ctx)
    xa, h = _prenorm(x[0], ctx[0], norm_g, mod4)

    def col_vec(pieces):
        return jnp.concatenate([jnp.broadcast_to(jnp.asarray(v, F32), (n,)) for v, n in pieces]).reshape(1, -1)

    bf = lambda w: w.astype(BF16)
    ab_w_in, ab_w_out, c_w_in, c_w_out = bf(ab_w_in), bf(ab_w_out), bf(c_w_in), bf(c_w_out)
    ffn_w1, ffn_w3, ffn_w2 = bf(ffn_w1), bf(ffn_w3), bf(ffn_w2)
    moe_w1, moe_w3, moe_w2 = bf(moe_w1), bf(moe_w3), bf(moe_w2)

    for i in range(depth):
        j = i // 2
        if i % 2 == 0:
            q0 = s5_width
            k0 = q0 + b_heads * HEAD_DIM
            v0 = k0 + b_kv * HEAD_DIM
            n_in = v0 + b_kv * HEAD_DIM
            proj = _inproj(h, ab_w_in, j, cos_t, sin_t,
                           jnp.ones((1, n_in), F32),
                           col_vec([(1.0, q0), (q_scale, k0 - q0), (1.0, n_in - k0)]),
                           q0, v0, False)
            s5w = _s5_weights(s5_lam_re[j], s5_lam_im[j], s5_log_dt[j], s5_b_re[j], s5_b_im[j],
                              s5_c_re[j], s5_c_im[j], s5_d[j])
            a = _s5_glu(_s5(proj[:, :s5_width].astype(F32), n_ctx, s5w), glu_w[j].astype(BF16), glu_b[j])
            o = _window_attention(proj, attn_sink[j], n_ctx, q0, k0, v0, b_kv, b_heads // b_kv)
            blocks = [(a, 0)] + [(o, c) for c in range(o.shape[1] // s5_width)]
            y = _matmul(blocks, ab_w_out, j, s5_width, (1280, 640, 256), 512, BF16, 48, "ab_out")
        else:
            k0 = c_heads * HEAD_DIM
            v0 = k0 + c_kv * HEAD_DIM
            n_in = v0 + c_kv * HEAD_DIM
            proj = _inproj(h, c_w_in, j, cos_t, sin_t,
                           col_vec([(jnp.tile(q_norm[j], c_heads), k0), (jnp.tile(k_norm[j], c_kv), v0 - k0),
                                    (1.0, n_in - v0)]),
                           col_vec([(q_scale, k0), (1.0, n_in - k0)]),
                           0, v0, True)
            o = _flash_attention(proj, n_ctx, 0, k0, v0, c_kv, c_heads // c_kv)
            y = _matmul([(o, 0)], c_w_out, j, o.shape[1], (1280, 640, 256), 512, BF16, 48, "c_out")
        moe_layer = i % 2 == 1
        res = _resid(xa, y, norm_g, mod4, i, 1, 2, i, 2, 3, packed=moe_layer)
        xa, h = res[0], res[1]
        if not moe_layer:
            g = _swiglu_up(h, ffn_w1, ffn_w3, j)
            f = _matmul([(g, 0)], ffn_w2, j, g.shape[1], (1280, 640, 256), 512, BF16, 56, "ffn_down")
            rows = None
        else:
            n_e = router_w.shape[2]
            src, wrow, blk_expert, n_used, rows = _moe_plan(_router(h, router_w[j]), n_e)
            f = _moe_experts(_moe_dispatch(res[2], src), wrow, blk_expert, n_used, moe_w1, moe_w3, moe_w2, j)
        if i + 1 < depth:
            xa, h = _resid(xa, f, norm_g, mod4, i, 3, 5, i + 1, 0, 0, rows=rows)
        else:
            out, = _resid(xa, f, norm_g, mod4, i, 3, 5, rows=rows, row_offset=n_ctx)
    return out[None]
```

```python
import functools
import math

import jax
import jax.numpy as jnp
from jax import lax
from jax.experimental import pallas as pl
from jax.experimental.pallas import tpu as pltpu

F32 = jnp.float32
BF16 = jnp.bfloat16

HEAD_DIM = 128
GRID_W = 64
ROPE_THETA = 10000.0
NORM_EPS = 1e-6
NEG = -1e30
N_MOD = 6
S5_CHUNK = 16
WINDOW = 128
LANES = 128
ROW_TILE = 256
MIB = 1024 * 1024


def _cp(sem, vmem_mib):
    return pltpu.CompilerParams(dimension_semantics=sem, vmem_limit_bytes=vmem_mib * MIB)


def _pick(n, cands):
    for t in cands:
        if n % t == 0:
            return t
    raise ValueError(f"no tile for {n} in {cands}")


def _rms(v):
    return v * lax.rsqrt(jnp.mean(v * v, axis=-1, keepdims=True) + NORM_EPS)


def _adaln_body(cond_ref, w_ref, b_ref, o_ref):
    c = cond_ref[...]
    a = (c * jax.nn.sigmoid(c)).astype(BF16)
    o_ref[...] = jnp.dot(a, w_ref[...].astype(BF16), preferred_element_type=F32) + b_ref[...]


def _adaln(cond8, w_mod, b_mod):
    depth, d, n6 = w_mod.shape
    tn = 512
    return pl.pallas_call(
        _adaln_body,
        grid=(depth, n6 // tn),
        in_specs=[pl.BlockSpec((8, d), lambda l, j: (0, 0)),
                  pl.BlockSpec((None, d, tn), lambda l, j: (l, 0, j)),
                  pl.BlockSpec((None, 1, tn), lambda l, j: (l, 0, j))],
        out_specs=pl.BlockSpec((None, 8, tn), lambda l, j: (l, 0, j)),
        out_shape=jax.ShapeDtypeStruct((depth, 8, n6), F32),
        compiler_params=_cp(("parallel", "parallel"), 40),
        name="adaln",
    )(cond8, w_mod, b_mod.reshape(depth, 1, n6))


def _prenorm_body(x_ref, ctx_ref, g_ref, mod_ref, xa_ref, h_ref):
    i = pl.program_id(0)

    def emit(v):
        xa_ref[...] = v
        n = _rms(v) * g_ref[0:1, :]
        h_ref[...] = (n * (1.0 + mod_ref[1:2, :]) + mod_ref[0:1, :]).astype(BF16)

    @pl.when(i == 0)
    def _():
        emit(ctx_ref[...])

    @pl.when(i > 0)
    def _():
        emit(x_ref[...])


def _prenorm(x2, ctx2, norm_g, mod4):
    seq, d = x2.shape
    nc = ctx2.shape[0]
    nt = seq + nc
    tr = ROW_TILE
    return pl.pallas_call(
        _prenorm_body,
        grid=(nt // tr,),
        in_specs=[pl.BlockSpec((tr, d), lambda i: (jnp.maximum(i - 1, 0), 0)),
                  pl.BlockSpec((tr, d), lambda i: (0, 0)),
                  pl.BlockSpec((None, 4, d), lambda i: (0, 0, 0)),
                  pl.BlockSpec((None, None, N_MOD, d), lambda i: (0, jnp.minimum(i, 1), 0, 0))],
        out_specs=[pl.BlockSpec((tr, d), lambda i: (i, 0)),
                   pl.BlockSpec((tr, d), lambda i: (i, 0))],
        out_shape=[jax.ShapeDtypeStruct((nt, d), F32), jax.ShapeDtypeStruct((nt, d), BF16)],
        compiler_params=_cp(("parallel",), 40),
        name="prenorm",
    )(x2, ctx2, norm_g, mod4)


def _row_gather_start(index_of, src_hbm, dst_ref, sem):
    def issue(r, carry):
        pltpu.make_async_copy(src_hbm.at[pl.ds(index_of(r), 1), :], dst_ref.at[pl.ds(r, 1), :], sem).start()
        return carry

    lax.fori_loop(0, dst_ref.shape[0], issue, 0, unroll=8)


def _row_gather_wait(src_hbm, dst_ref, sem):
    pltpu.make_async_copy(src_hbm.at[pl.ds(0, dst_ref.shape[0]), :], dst_ref, sem).wait()


def _pack_halves(v):
    half = v.shape[1] // 2
    vb = v.astype(BF16).astype(F32)
    hi = lax.bitcast_convert_type(vb[:, :half], jnp.uint32)
    lo = lax.bitcast_convert_type(vb[:, half:], jnp.uint32)
    return (hi & jnp.uint32(0xFFFF0000)) | (lo >> jnp.uint32(16))


def _unpack_halves(u):
    hi = lax.bitcast_convert_type(u & jnp.uint32(0xFFFF0000), F32)
    lo = lax.bitcast_convert_type(u << jnp.uint32(16), F32)
    return jnp.concatenate([hi.astype(BF16), lo.astype(BF16)], axis=1)


def _resid_body(*refs, post_idx, gate_idx, next_g_idx, next_shift_idx, gathered, packed, last):
    refs = list(refs)
    if gathered:
        p_ref = refs.pop(0)
    xa_ref, y_ref, gcur_ref, mcur_ref = refs[:4]
    refs = refs[4:]
    if not last:
        gnext_ref, mnext_ref = refs[:2]
        refs = refs[2:]
    xo_ref = refs.pop(0)
    if gathered:
        buf_ref, sem = refs[-2:]
        _row_gather_start(lambda r: p_ref[0, r], y_ref, buf_ref.at[0], sem)
        _row_gather_start(lambda r: p_ref[1, r], y_ref, buf_ref.at[1], sem)
        _row_gather_wait(y_ref, buf_ref.at[0], sem)
        _row_gather_wait(y_ref, buf_ref.at[1], sem)
        y = buf_ref[0] + buf_ref[1]
    else:
        y = y_ref[...].astype(F32)
    x = xa_ref[...] + mcur_ref[gate_idx:gate_idx + 1, :] * (_rms(y) * gcur_ref[post_idx:post_idx + 1, :])
    xo_ref[...] = x
    if last:
        return
    n = _rms(x) * gnext_ref[next_g_idx:next_g_idx + 1, :]
    h = (n * (1.0 + mnext_ref[next_shift_idx + 1:next_shift_idx + 2, :])
         + mnext_ref[next_shift_idx:next_shift_idx + 1, :])
    refs[0][...] = h.astype(BF16)
    if packed:
        refs[1][...] = _pack_halves(h)


def _resid(xa, y, norm_g, mod4, layer, post_idx, gate_idx, next_layer=None, next_g_idx=0, next_shift_idx=0,
           rows=None, packed=False, row_offset=0):
    nt, d = xa.shape
    tr = ROW_TILE
    off = row_offset // tr
    n_blk = (nt - row_offset) // tr
    last = next_layer is None
    gathered = rows is not None
    row = lambda i: (i + off, 0)
    mod_spec = lambda l: pl.BlockSpec((None, None, N_MOD, d), lambda i: (l, jnp.minimum(i + off, 1), 0, 0))
    in_specs, args = [], []
    if gathered:
        in_specs.append(pl.BlockSpec((None, 2, tr), lambda i: (i + off, 0, 0), memory_space=pltpu.SMEM))
        args.append(rows.reshape(2, nt // tr, tr).transpose(1, 0, 2))
    in_specs += [pl.BlockSpec((tr, d), row),
                 pl.BlockSpec(memory_space=pl.ANY) if gathered else pl.BlockSpec((tr, d), row),
                 pl.BlockSpec((None, 4, d), lambda i: (layer, 0, 0)), mod_spec(layer)]
    args += [xa, y, norm_g, mod4]
    out_specs = [pl.BlockSpec((tr, d), lambda i: (i, 0) if last else (i + off, 0))]
    out_shape = [jax.ShapeDtypeStruct((nt - row_offset if last else nt, d), F32)]
    if not last:
        in_specs += [pl.BlockSpec((None, 4, d), lambda i: (next_layer, 0, 0)), mod_spec(next_layer)]
        args += [norm_g, mod4]
        out_specs.append(pl.BlockSpec((tr, d), row))
        out_shape.append(jax.ShapeDtypeStruct((nt, d), BF16))
        if packed:
            out_specs.append(pl.BlockSpec((tr, d // 2), row))
            out_shape.append(jax.ShapeDtypeStruct((nt, d // 2), jnp.uint32))
    scratch = [pltpu.VMEM((2, tr, d), F32), pltpu.SemaphoreType.DMA(())] if gathered else []
    body = functools.partial(_resid_body, post_idx=post_idx, gate_idx=gate_idx, next_g_idx=next_g_idx,
                             next_shift_idx=next_shift_idx, gathered=gathered, packed=packed, last=last)
    return pl.pallas_call(
        body,
        grid=(n_blk,),
        in_specs=in_specs,
        out_specs=out_specs,
        out_shape=out_shape,
        scratch_shapes=scratch,
        input_output_aliases={} if last else {(1 if gathered else 0): 0},
        compiler_params=_cp(("arbitrary",) if gathered else ("parallel",), 48),
        name="resid",
    )(*args)


def _inproj_body(h_ref, w_ref, cos_ref, sin_ref, gain_ref, scale_ref, o_ref, *, rope_lo, rope_hi, use_norm):
    j = pl.program_id(1)
    acc = jnp.dot(h_ref[...], w_ref[...], preferred_element_type=F32)
    is_rope = jnp.logical_and(j >= rope_lo, j < rope_hi)

    @pl.when(jnp.logical_not(is_rope))
    def _():
        o_ref[...] = acc.astype(BF16)

    @pl.when(is_rope)
    def _():
        c = cos_ref[...]
        s = sin_ref[...]
        lane = lax.broadcasted_iota(jnp.int32, c.shape, 1)
        first = (lane & 32) == 0
        for hh in range(acc.shape[1] // HEAD_DIM):
            sl = slice(hh * HEAD_DIM, (hh + 1) * HEAD_DIM)
            blk = acc[:, sl]
            if use_norm:
                blk = _rms(blk) * gain_ref[:, sl]
            rot = jnp.where(first, pltpu.roll(blk, HEAD_DIM - 32, 1), pltpu.roll(blk, 32, 1))
            o_ref[:, sl] = ((blk * c + rot * s) * scale_ref[:, sl]).astype(BF16)


def _inproj(h, w, layer, cos_t, sin_t, gain, scale, rope_lo, rope_hi, use_norm):
    nt, d = h.shape
    n = w.shape[2]
    tm = _pick(nt, (1280, 640, 256))
    tn = 512
    body = functools.partial(_inproj_body, rope_lo=rope_lo // tn, rope_hi=rope_hi // tn, use_norm=use_norm)
    return pl.pallas_call(
        body,
        grid=(nt // tm, n // tn),
        in_specs=[pl.BlockSpec((tm, d), lambda i, j: (i, 0)),
                  pl.BlockSpec((None, d, tn), lambda i, j: (layer, 0, j)),
                  pl.BlockSpec((tm, HEAD_DIM), lambda i, j: (i, 0)),
                  pl.BlockSpec((tm, HEAD_DIM), lambda i, j: (i, 0)),
                  pl.BlockSpec((1, tn), lambda i, j: (0, j)),
                  pl.BlockSpec((1, tn), lambda i, j: (0, j))],
        out_specs=pl.BlockSpec((tm, tn), lambda i, j: (i, j)),
        out_shape=jax.ShapeDtypeStruct((nt, n), BF16),
        compiler_params=_cp(("parallel", "arbitrary"), 48),
        name="inproj",
    )(h, w, cos_t, sin_t, gain, scale)


def _mm_body(*refs, n_a):
    o_ref = refs[2 * n_a]
    acc = jnp.dot(refs[0][...], refs[n_a][...], preferred_element_type=F32)
    for t in range(1, n_a):
        acc += jnp.dot(refs[t][...], refs[n_a + t][...], preferred_element_type=F32)
    o_ref[...] = acc.astype(o_ref.dtype)


def _matmul(a_blocks, w, layer, kw, tm_cands, tn, out_dtype, vmem_mib, name):
    nt = a_blocks[0][0].shape[0]
    n = w.shape[2]
    tm = _pick(nt, tm_cands)
    in_specs = [pl.BlockSpec((tm, kw), lambda i, j, c=c: (i, c)) for _, c in a_blocks]
    in_specs += [pl.BlockSpec((None, kw, tn), lambda i, j, t=t: (layer, t, j)) for t in range(len(a_blocks))]
    return pl.pallas_call(
        functools.partial(_mm_body, n_a=len(a_blocks)),
        grid=(nt // tm, n // tn),
        in_specs=in_specs,
        out_specs=pl.BlockSpec((tm, tn), lambda i, j: (i, j)),
        out_shape=jax.ShapeDtypeStruct((nt, n), out_dtype),
        compiler_params=_cp(("parallel", "arbitrary"), vmem_mib),
        name=name,
    )(*[a for a, _ in a_blocks], *([w] * len(a_blocks)))


def _swiglu_up_body(h_ref, w1_ref, w3_ref, o_ref):
    h = h_ref[...]
    a = jnp.dot(h, w1_ref[...], preferred_element_type=F32)
    b = jnp.dot(h, w3_ref[...], preferred_element_type=F32)
    o_ref[...] = (a * jax.nn.sigmoid(a) * b).astype(BF16)


def _swiglu_up(h, w1, w3, layer):
    nt, d = h.shape
    f = w1.shape[2]
    tm, tn = _pick(nt, (1280, 640, 256)), 512
    w_spec = pl.BlockSpec((None, d, tn), lambda i, j: (layer, 0, j))
    return pl.pallas_call(
        _swiglu_up_body,
        grid=(nt // tm, f // tn),
        in_specs=[pl.BlockSpec((tm, d), lambda i, j: (i, 0)), w_spec, w_spec],
        out_specs=pl.BlockSpec((tm, tn), lambda i, j: (i, j)),
        out_shape=jax.ShapeDtypeStruct((nt, f), BF16),
        compiler_params=_cp(("parallel", "arbitrary"), 52),
        name="swiglu_up",
    )(h, w1, w3)


def _router_body(h_ref, whi_ref, wlo_ref, g_ref, *, n_experts):
    h = h_ref[...]
    logits = (jnp.dot(h, whi_ref[...], preferred_element_type=F32)
              + jnp.dot(h, wlo_ref[...], preferred_element_type=F32))
    lane = lax.broadcasted_iota(jnp.int32, logits.shape, 1)
    lg = jnp.where(lane < n_experts, logits, NEG)
    v1 = jnp.max(lg, axis=-1, keepdims=True)
    i1 = jnp.min(jnp.where(lg == v1, lane, LANES), axis=-1, keepdims=True)
    lg2 = jnp.where(lane == i1, NEG, lg)
    v2 = jnp.max(lg2, axis=-1, keepdims=True)
    i2 = jnp.min(jnp.where(lg2 == v2, lane, LANES), axis=-1, keepdims=True)
    e = jnp.exp(v2 - v1)
    den = 1.0 + e
    g_ref[...] = (jnp.where(lane == i1, 1.0 / den, 0.0) + jnp.where(lane == i2, e / den, 0.0)
                  + jnp.where(lane == n_experts, i1.astype(F32), 0.0)
                  + jnp.where(lane == n_experts + 1, i2.astype(F32), 0.0))


def _router(h, router_w):
    nt, d = h.shape
    n_e = router_w.shape[1]
    whi = router_w.astype(BF16)
    wlo = (router_w - whi.astype(F32)).astype(BF16)
    pad = ((0, 0), (0, LANES - n_e))
    whi = jnp.pad(whi, pad)
    wlo = jnp.pad(wlo, pad)
    tm = _pick(nt, (640, 256))
    return pl.pallas_call(
        functools.partial(_router_body, n_experts=n_e),
        grid=(nt // tm,),
        in_specs=[pl.BlockSpec((tm, d), lambda i: (i, 0)),
                  pl.BlockSpec((d, LANES), lambda i: (0, 0)),
                  pl.BlockSpec((d, LANES), lambda i: (0, 0))],
        out_specs=pl.BlockSpec((tm, LANES), lambda i: (i, 0)),
        out_shape=jax.ShapeDtypeStruct((nt, LANES), F32),
        compiler_params=_cp(("parallel",), 32),
        name="router",
    )(h, whi, wlo)


MOE_ROWS = 512


def _moe_plan(gates, n_e):
    nt = gates.shape[0]
    tg = MOE_ROWS
    i1 = gates[:, n_e].astype(jnp.int32)
    i2 = gates[:, n_e + 1].astype(jnp.int32)
    w1 = jnp.take_along_axis(gates, i1[:, None], axis=1)[:, 0]
    w2 = jnp.take_along_axis(gates, i2[:, None], axis=1)[:, 0]
    experts = jnp.arange(n_e, dtype=jnp.int32)
    chosen = ((i1[:, None] == experts) | (i2[:, None] == experts)).astype(jnp.int32)
    rank = jnp.cumsum(chosen, axis=0) - chosen
    padded = (jnp.sum(chosen, axis=0) + tg - 1) // tg * tg
    ends = jnp.cumsum(padded)
    pos = (ends - padded)[None, :] + rank
    p1 = jnp.take_along_axis(pos, i1[:, None], axis=1)[:, 0]
    p2 = jnp.take_along_axis(pos, i2[:, None], axis=1)[:, 0]
    n_rows = 2 * nt + n_e * tg
    token = jnp.arange(nt, dtype=jnp.int32)
    pairs = jnp.stack([jnp.concatenate([token, token]).astype(F32), jnp.concatenate([w1, w2])], axis=1)
    table = jnp.zeros((n_rows, 2), F32).at[jnp.concatenate([p1, p2])].set(pairs)
    src, wrow = table[:, 0].astype(jnp.int32), table[:, 1]
    blk_start = jnp.arange(n_rows // tg, dtype=jnp.int32) * tg
    blk_expert = jnp.minimum(jnp.searchsorted(ends, blk_start, side='right'), n_e - 1).astype(jnp.int32)
    n_used = (ends[-1] // tg).astype(jnp.int32).reshape(1)
    return src, wrow, blk_expert, n_used, jnp.stack([p1, p2]).astype(jnp.int32)


def _dispatch_body(src_ref, hp_hbm, o_ref, sem):
    _row_gather_start(lambda r: src_ref[0, r], hp_hbm, o_ref, sem)
    _row_gather_wait(hp_hbm, o_ref, sem)


def _moe_dispatch(hp, src):
    n_rows, tg = src.shape[0], MOE_ROWS
    return pl.pallas_call(
        _dispatch_body,
        grid=(n_rows // tg,),
        in_specs=[pl.BlockSpec((None, 1, tg), lambda b: (b, 0, 0), memory_space=pltpu.SMEM),
                  pl.BlockSpec(memory_space=pl.ANY)],
        out_specs=pl.BlockSpec((tg, hp.shape[1]), lambda b: (b, 0)),
        out_shape=jax.ShapeDtypeStruct((n_rows, hp.shape[1]), jnp.uint32),
        scratch_shapes=[pltpu.SemaphoreType.DMA(())],
        compiler_params=_cp(("arbitrary",), 32),
        name="moe_dispatch",
    )(src.reshape(n_rows // tg, 1, tg), hp)


def _moe_up_body(expert_ref, used_ref, x_ref, w1_ref, w3_ref, gate_ref, o_ref):
    b = pl.program_id(0)

    @pl.when(b < used_ref[0])
    def _():
        x = _unpack_halves(x_ref[...])
        a = jnp.dot(x, w1_ref[...], preferred_element_type=F32)
        c = jnp.dot(x, w3_ref[...], preferred_element_type=F32)
        o_ref[...] = (a * jax.nn.sigmoid(a) * c * gate_ref[...]).astype(BF16)

    @pl.when(b >= used_ref[0])
    def _():
        o_ref[...] = jnp.zeros(o_ref.shape, o_ref.dtype)


def _moe_down_body(expert_ref, used_ref, g_ref, w2_ref, o_ref):
    b = pl.program_id(0)

    @pl.when(b < used_ref[0])
    def _():
        o_ref[...] = jnp.dot(g_ref[...], w2_ref[...], preferred_element_type=F32)

    @pl.when(b >= used_ref[0])
    def _():
        o_ref[...] = jnp.zeros(o_ref.shape, o_ref.dtype)


def _moe_experts(xs, wrow, blk_expert, n_used, w1, w3, w2, layer):
    n_rows, tg = xs.shape[0], MOE_ROWS
    _, n_e, d, f = w1.shape
    row = lambda b, e, u: (b, 0)
    per_expert = lambda b, e, u: (layer, e[b], 0, 0)
    g = pl.pallas_call(
        _moe_up_body,
        grid_spec=pltpu.PrefetchScalarGridSpec(
            num_scalar_prefetch=2, grid=(n_rows // tg,),
            in_specs=[pl.BlockSpec((tg, xs.shape[1]), row), pl.BlockSpec((None, None, d, f), per_expert),
                      pl.BlockSpec((None, None, d, f), per_expert), pl.BlockSpec((tg, 1), row)],
            out_specs=pl.BlockSpec((tg, f), row)),
        out_shape=jax.ShapeDtypeStruct((n_rows, f), BF16),
        compiler_params=_cp(("arbitrary",), 56),
        name="moe_up",
    )(blk_expert, n_used, xs, w1, w3, wrow.reshape(n_rows, 1))
    return pl.pallas_call(
        _moe_down_body,
        grid_spec=pltpu.PrefetchScalarGridSpec(
            num_scalar_prefetch=2, grid=(n_rows // tg,),
            in_specs=[pl.BlockSpec((tg, f), row), pl.BlockSpec((None, None, f, d), per_expert)],
            out_specs=pl.BlockSpec((tg, d), row)),
        out_shape=jax.ShapeDtypeStruct((n_rows, d), F32),
        compiler_params=_cp(("arbitrary",), 48),
        name="moe_down",
    )(blk_expert, n_used, g, w2)


V_ROWS = HEAD_DIM + 16
LOG2E = math.log2(math.e)


def _stack_heads(q, n_grp):
    return jnp.concatenate([q[:, g * HEAD_DIM:(g + 1) * HEAD_DIM] for g in range(n_grp)], axis=0)


def _unstack_store(o_ref, o, n_grp, rows):
    for g in range(n_grp):
        o_ref[:, g * HEAD_DIM:(g + 1) * HEAD_DIM] = o[g * rows:(g + 1) * rows, :].astype(o_ref.dtype)


def _scores_t(k, q):
    return lax.dot_general(k, q, (((1,), (1,)), ((), ())), preferred_element_type=F32)


def _transpose_v(v, n_kv):
    nt = v.shape[0]
    vt = v.reshape(nt, n_kv, HEAD_DIM).transpose(1, 2, 0)
    return jnp.concatenate([vt, jnp.ones((n_kv, V_ROWS - HEAD_DIM, nt), v.dtype)], axis=1)


def _win_body(sink_ref, q_ref, km_ref, k0_ref, kp_ref, kc_ref, vm_ref, v0_ref, vp_ref, vc_ref, o_ref, *,
              n_grp, ctx_blocks, n_lat):
    kvh = pl.program_id(0)
    n = pl.program_id(1)
    blk = q_ref.shape[0]
    qs = _stack_heads(q_ref[...], n_grp)
    keys = jnp.concatenate([km_ref[...], k0_ref[...], kp_ref[...], kc_ref[...]], axis=0)
    vt = jnp.concatenate([vm_ref[...], v0_ref[...], vp_ref[...], vc_ref[...]], axis=1)
    st = _scores_t(keys, qs)
    n_win = 3 * blk
    k_off = lax.broadcasted_iota(jnp.int32, st.shape, 0) - blk
    q_off = lax.broadcasted_iota(jnp.int32, st.shape, 1) % blk
    kpos = (n - ctx_blocks) * blk + k_off
    in_band = ((jnp.abs(q_off - k_off) <= WINDOW) & (kpos >= 0) & (kpos < n_lat) & (n >= ctx_blocks))
    st = jnp.where(in_band | (k_off >= n_win - blk), st, NEG)
    sink = jnp.concatenate(
        [jnp.full((1, blk), sink_ref[kvh * n_grp + g] * LOG2E, F32) for g in range(n_grp)], axis=1)
    m = jnp.maximum(jnp.max(st, axis=0, keepdims=True), sink)
    pt = jnp.exp2(st - m).astype(BF16)
    acc = jnp.dot(vt, pt, preferred_element_type=F32)
    den = acc[HEAD_DIM:HEAD_DIM + 1, :] + jnp.exp2(sink - m)
    _unstack_store(o_ref, (acc[0:HEAD_DIM, :] / den).T, n_grp, blk)


def _window_attention(proj, sink, n_ctx, q_col, k_col, v_col, n_kv, n_grp):
    nt = proj.shape[0]
    blk = WINDOW
    nb = nt // blk
    cb = n_ctx // blk
    qc, kc = q_col // (n_grp * HEAD_DIM), k_col // HEAD_DIM
    vt = _transpose_v(proj[:, v_col:v_col + n_kv * HEAD_DIM], n_kv)
    lo = lambda n: jnp.maximum(n - 1, cb)
    hi = lambda n: jnp.minimum(n + 1, nb - 1)
    same = lambda n: n
    k_tile = lambda rowf: pl.BlockSpec((blk, HEAD_DIM), lambda h, n: (rowf(n), kc + h))
    v_tile = lambda rowf: pl.BlockSpec((None, V_ROWS, blk), lambda h, n: (h, 0, rowf(n)))
    body = functools.partial(_win_body, n_grp=n_grp, ctx_blocks=cb, n_lat=nt - n_ctx)
    return pl.pallas_call(
        body,
        grid=(n_kv, nb),
        in_specs=[pl.BlockSpec(memory_space=pltpu.SMEM),
                  pl.BlockSpec((blk, n_grp * HEAD_DIM), lambda h, n: (n, qc + h)),
                  k_tile(lo), k_tile(same), k_tile(hi),
                  pl.BlockSpec((n_ctx, HEAD_DIM), lambda h, n: (0, kc + h)),
                  v_tile(lo), v_tile(same), v_tile(hi),
                  pl.BlockSpec((None, V_ROWS, n_ctx), lambda h, n: (h, 0, 0))],
        out_specs=pl.BlockSpec((blk, n_grp * HEAD_DIM), lambda h, n: (n, h)),
        out_shape=jax.ShapeDtypeStruct((nt, n_kv * n_grp * HEAD_DIM), BF16),
        compiler_params=_cp(("parallel", "parallel"), 32),
        name="window_attn",
    )(sink, proj, proj, proj, proj, proj, vt, vt, vt, vt)


def _flash_body(q_ref, k_ref, vt_ref, o_ref, s_ref, acc_ref, *, n_grp, tk, n_chunks, n_ctx):
    qi = pl.program_id(1)
    tq = q_ref.shape[0]
    qs = _stack_heads(q_ref[...], n_grp)
    is_ctx = qi == 0

    def scores(c):
        start = c * tk if isinstance(c, int) else pl.multiple_of(c * tk, tk)
        return _scores_t(k_ref[pl.ds(start, tk), :], qs)

    def absorb(c, slot, m_old):
        m_new = jnp.maximum(m_old, jnp.max(s_ref[slot], axis=0, keepdims=True))
        alpha = jnp.exp2(m_old - m_new)
        pt = jnp.exp2(s_ref[slot] - m_new).astype(BF16)
        acc_ref[...] = alpha * acc_ref[...] + jnp.dot(vt_ref[c], pt, preferred_element_type=F32)
        return m_new

    acc_ref[...] = jnp.zeros(acc_ref.shape, F32)
    st0 = scores(0)
    row = lax.broadcasted_iota(jnp.int32, st0.shape, 0)
    s_ref[0] = jnp.where(row < jnp.where(is_ctx, n_ctx, tk), st0, NEG)
    m = jnp.full((1, qs.shape[0]), NEG, F32)

    def pair(j, m):
        c = 2 * j
        s_ref[1] = scores(c + 1)
        m = absorb(c, 0, m)
        s_ref[0] = scores(c + 2)
        return absorb(c + 1, 1, m)

    n_pairs = (n_chunks - 1) // 2
    m = lax.fori_loop(0, jnp.where(is_ctx, 0, n_pairs), pair, m)
    if (n_chunks - 1) % 2 == 0:
        absorb(jnp.where(is_ctx, 0, n_chunks - 1), 0, m)
    else:
        s_ref[1] = jnp.where(is_ctx, NEG, scores(n_chunks - 1))
        m = absorb(jnp.where(is_ctx, 0, n_chunks - 2), 0, m)
        absorb(n_chunks - 1, 1, m)
    acc = acc_ref[...]
    _unstack_store(o_ref, (acc[0:HEAD_DIM, :] / acc[HEAD_DIM:HEAD_DIM + 1, :]).T, n_grp, tq)


def _flash_attention(proj, n_ctx, q_col, k_col, v_col, n_kv, n_grp):
    nt = proj.shape[0]
    tq = n_ctx
    tk = _pick(nt, (1280, 256))
    n_chunks = nt // tk
    qc, kc = q_col // (n_grp * HEAD_DIM), k_col // HEAD_DIM
    vt = _transpose_v(proj[:, v_col:v_col + n_kv * HEAD_DIM], n_kv)
    vt = vt.reshape(n_kv, V_ROWS, n_chunks, tk).transpose(0, 2, 1, 3)
    body = functools.partial(_flash_body, n_grp=n_grp, tk=tk, n_chunks=n_chunks, n_ctx=n_ctx)
    return pl.pallas_call(
        body,
        grid=(n_kv, nt // tq),
        in_specs=[pl.BlockSpec((tq, n_grp * HEAD_DIM), lambda h, i: (i, qc + h)),
                  pl.BlockSpec((nt, HEAD_DIM), lambda h, i: (0, kc + h)),
                  pl.BlockSpec((None, n_chunks, V_ROWS, tk), lambda h, i: (h, 0, 0, 0))],
        out_specs=pl.BlockSpec((tq, n_grp * HEAD_DIM), lambda h, i: (i, h)),
        out_shape=jax.ShapeDtypeStruct((nt, n_kv * n_grp * HEAD_DIM), BF16),
        scratch_shapes=[pltpu.VMEM((2, tk, n_grp * tq), F32), pltpu.VMEM((V_ROWS, n_grp * tq), F32)],
        compiler_params=_cp(("parallel", "parallel"), 56),
        name="flash_attn",
    )(proj, proj, vt)


def _s5_weights(lam_re, lam_im, log_dt, b_re, b_im, c_re, c_im, d_skip):
    hp = lax.Precision.HIGH
    n_g, n_p = lam_re.shape[1], lam_re.shape[2]
    n_h = b_re.shape[-1]
    ell = S5_CHUNK
    dt = jnp.exp(log_dt)[..., None]
    ar, ai = lam_re * dt, lam_im * dt
    k = jnp.arange(ell + 1, dtype=F32)[:, None, None, None]
    mag = jnp.exp(k * ar[None])
    lp_re, lp_im = mag * jnp.cos(k * ai[None]), mag * jnp.sin(k * ai[None])
    nr, ni = lp_re[1] - 1.0, lp_im[1]
    den = lam_re * lam_re + lam_im * lam_im
    fr, fi = (nr * lam_re + ni * lam_im) / den, (ni * lam_re - nr * lam_im) / den
    bb_re = fr[..., None] * b_re - fi[..., None] * b_im
    bb_im = fr[..., None] * b_im + fi[..., None] * b_re
    lpe_re, lpe_im = lp_re[:, :, :, None, :], lp_im[:, :, :, None, :]
    cl_re = c_re[None] * lpe_re - c_im[None] * lpe_im
    cl_im = c_re[None] * lpe_im + c_im[None] * lpe_re
    m = (jnp.einsum('kdghp,dgpi->kdghi', cl_re, bb_re, precision=hp)
         - jnp.einsum('kdghp,dgpi->kdghi', cl_im, bb_im, precision=hp))
    gl = LANES // n_h
    n_q = n_g // gl
    lane = jnp.arange(LANES)
    put_h = (lane[None, None, :] == (jnp.arange(gl)[:, None, None] * n_h
                                     + jnp.arange(n_h)[None, :, None])).astype(F32)
    put_p = (lane[None, None, :] == (jnp.arange(2)[:, None, None] * n_p
                                     + jnp.arange(n_p)[None, :, None])).astype(F32)
    m_lag = jnp.concatenate([m[ell - 1:0:-1, 1], (m[0, 0] + m[0, 1])[None], m[1:ell, 0]], axis=0)
    m_lag = m_lag.reshape(2 * ell - 1, n_q, gl, n_h, n_h)
    toep_c = jnp.einsum('dqghi,gir,ghl->dqrl', m_lag, put_h, put_h)

    def state_in(direction, powers):
        pr = lp_re[powers, direction].transpose(1, 0, 2)[:, :, None, :]
        pi = lp_im[powers, direction].transpose(1, 0, 2)[:, :, None, :]
        br = bb_re[direction].transpose(0, 2, 1)[:, None, :, :]
        bi = bb_im[direction].transpose(0, 2, 1)[:, None, :, :]
        return pr * br - pi * bi, pr * bi + pi * br

    steps = jnp.arange(ell)
    f_re, f_im = state_in(0, ell - 1 - steps)
    g_re, g_im = state_in(1, steps)
    w_in = jnp.stack([f_re, f_im, g_re, g_im], axis=3)
    w_in = w_in.reshape(n_q, gl // 2, 2, ell, n_h, 4, n_p)
    w_in_c = jnp.einsum('qrsaikp,spl->qarskil', w_in, put_p)
    w_in_c = w_in_c.reshape(n_q, ell, gl, 4, n_h, LANES)

    def state_out(direction, powers):
        wr = cl_re[powers, direction].transpose(1, 3, 0, 2)
        wi = -cl_im[powers, direction].transpose(1, 3, 0, 2)
        w = jnp.stack([wr, wi], axis=1).reshape(n_q, gl, 2, n_p, ell, n_h)
        return jnp.einsum('qgepth,ghl->qgetpl', w, put_h)

    w_of_c = state_out(0, steps + 1)
    w_ob_c = state_out(1, ell - steps)
    lam_l = jnp.stack([lp_re[ell, 0], lp_im[ell, 0], lp_re[ell, 1], lp_im[ell, 1]], axis=0)
    lam_l = lam_l.reshape(4, n_g // 2, 2 * n_p)
    d_q = jnp.broadcast_to(d_skip.reshape(n_q, 1, gl, n_h), (n_q, ell, gl, n_h)).reshape(n_q, 1, ell * LANES)
    return toep_c.astype(BF16), w_in_c.astype(BF16), w_of_c.astype(BF16), w_ob_c.astype(BF16), lam_l, d_q


def _s5_build_toeplitz(toep_ref, t_ref):
    ell = S5_CHUNK
    for a in range(ell):
        for b in range(ell):
            t_ref[a * LANES:(a + 1) * LANES, b * LANES:(b + 1) * LANES] = toep_ref[b - a + ell - 1]


def _s5_build_state_in(w_ref, t_ref):
    _, gl, kinds, n_h, _ = w_ref.shape
    t_ref[...] = jnp.zeros(t_ref.shape, t_ref.dtype)
    for a in range(S5_CHUNK):
        for g in range(gl):
            r0 = (a * gl + g) * n_h
            for k in range(kinds):
                c0 = ((g // 2) * kinds + k) * LANES
                t_ref[r0:r0 + n_h, c0:c0 + LANES] = w_ref[a, g, k]


def _s5_build_state_out(w_ref, t_ref):
    gl, _, _, n_p, _ = w_ref.shape
    for g in range(gl):
        for e in range(2):
            r0 = ((g // 2) * 4 + e * 2 + g % 2) * n_p
            for t in range(S5_CHUNK):
                t_ref[r0:r0 + n_p, t * LANES:(t + 1) * LANES] = w_ref[g, e, t]


def _s5_chunk_rows(u_ref, n_rows):
    return jnp.concatenate([u_ref[pl.ds(t, n_rows, stride=S5_CHUNK), :] for t in range(S5_CHUNK)], axis=1)


def _s5_in_body(u_ref, w_ref, x_ref, win_ref):
    @pl.when(pl.program_id(1) == 0)
    def _():
        _s5_build_state_in(w_ref, win_ref)

    uc = _s5_chunk_rows(u_ref, x_ref.shape[0]).astype(BF16)
    x_ref[...] = jnp.dot(uc, win_ref[...], preferred_element_type=F32)


def _s5_scan_body(xf_ref, xb_ref, lam_ref, sf_ref, sb_ref, st_ref):
    @pl.when(pl.program_id(0) == 0)
    def _():
        st_ref[...] = jnp.zeros(st_ref.shape, F32)

    half = xf_ref.shape[2] // 2
    rows = xf_ref.shape[0]

    def run(x_ref, s_ref, base, order):
        ar, ai = lam_ref[base], lam_ref[base + 1]
        sr, si = st_ref[base], st_ref[base + 1]
        for r in order:
            s_ref[r, :, 0:half] = sr
            s_ref[r, :, half:2 * half] = si
            xr, xi = x_ref[r, :, 0:half], x_ref[r, :, half:2 * half]
            sr, si = ar * sr - ai * si + xr, ar * si + ai * sr + xi
        st_ref[base] = sr
        st_ref[base + 1] = si

    run(xf_ref, sf_ref, 0, range(rows))
    run(xb_ref, sb_ref, 2, range(rows - 1, -1, -1))


def _s5_out_body(u_ref, sf_ref, sb_ref, toep_ref, wof_ref, wob_ref, d_ref, y_ref, t_ref, of_ref, ob_ref):
    @pl.when(pl.program_id(1) == 0)
    def _():
        _s5_build_toeplitz(toep_ref, t_ref)
        _s5_build_state_out(wof_ref, of_ref)
        _s5_build_state_out(wob_ref, ob_ref)

    n_rows = sf_ref.shape[0]
    uc = _s5_chunk_rows(u_ref, n_rows)
    y = jnp.dot(uc.astype(BF16), t_ref[...], preferred_element_type=F32)
    y += jnp.dot(sf_ref[...].astype(BF16), of_ref[...], preferred_element_type=F32)
    y += jnp.dot(sb_ref[...].astype(BF16), ob_ref[...], preferred_element_type=F32)
    y += uc * d_ref[...]
    for t in range(S5_CHUNK):
        y_ref[pl.ds(t, n_rows, stride=S5_CHUNK), :] = y[:, t * LANES:(t + 1) * LANES]


def _s5(u, n_ctx, weights):
    toep_c, w_in_c, w_of_c, w_ob_c, lam_l, d_q = weights
    nt, width = u.shape
    ell = S5_CHUNK
    n_q, _, gl, kinds, _, _ = w_in_c.shape
    kdim = ell * LANES
    n_pair = lam_l.shape[1]
    sdim = kinds * LANES
    sdim_q = (gl // 2) * sdim
    n_chunk = nt // ell
    cb = _pick(n_chunk, (208, 240, 80))
    u_spec = pl.BlockSpec((cb * ell, LANES), lambda q, i: (i, q))

    def per_q(a):
        return pl.BlockSpec((None,) + a.shape[1:], lambda q, i: (q,) + (0,) * (a.ndim - 1))

    x = pl.pallas_call(
        _s5_in_body,
        grid=(n_q, n_chunk // cb),
        in_specs=[u_spec, per_q(w_in_c)],
        out_specs=pl.BlockSpec((cb, sdim_q), lambda q, i: (i, q)),
        out_shape=jax.ShapeDtypeStruct((n_chunk, n_q * sdim_q), F32),
        scratch_shapes=[pltpu.VMEM((kdim, sdim_q), BF16)],
        compiler_params=_cp(("arbitrary", "arbitrary"), 40),
        name="s5_state_in",
    )(u, w_in_c)
    xt = x.reshape(n_chunk, n_pair, sdim)
    rb = n_ctx // ell
    nb = n_chunk // rb
    half = sdim // 2
    bwd = lambda b: (jnp.where(b == 0, 0, nb - b), 0, 0)
    sf, sb = pl.pallas_call(
        _s5_scan_body,
        grid=(nb,),
        in_specs=[pl.BlockSpec((rb, n_pair, half), lambda b: (b, 0, 0)),
                  pl.BlockSpec((rb, n_pair, half), lambda b: (jnp.where(b == 0, 0, nb - b), 0, 1)),
                  pl.BlockSpec((4, n_pair, half // 2), lambda b: (0, 0, 0))],
        out_specs=[pl.BlockSpec((rb, n_pair, half), lambda b: (b, 0, 0)),
                   pl.BlockSpec((rb, n_pair, half), bwd)],
        out_shape=[jax.ShapeDtypeStruct((n_chunk, n_pair, half), F32),
                   jax.ShapeDtypeStruct((n_chunk, n_pair, half), F32)],
        scratch_shapes=[pltpu.VMEM((4, n_pair, half // 2), F32)],
        compiler_params=_cp(("arbitrary",), 32),
        name="s5_scan",
    )(xt, xt, lam_l)
    sdim_h = sdim_q // 2
    s_spec = pl.BlockSpec((cb, sdim_h), lambda q, i: (i, q))
    toep_spec = pl.BlockSpec((toep_c.shape[0], None, LANES, LANES), lambda q, i: (0, q, 0, 0))
    return pl.pallas_call(
        _s5_out_body,
        grid=(n_q, n_chunk // cb),
        in_specs=[u_spec, s_spec, s_spec, toep_spec, per_q(w_of_c), per_q(w_ob_c), per_q(d_q)],
        out_specs=u_spec,
        out_shape=jax.ShapeDtypeStruct((nt, width), F32),
        scratch_shapes=[pltpu.VMEM((kdim, kdim), BF16), pltpu.VMEM((sdim_h, kdim), BF16),
                        pltpu.VMEM((sdim_h, kdim), BF16)],
        compiler_params=_cp(("arbitrary", "arbitrary"), 56),
        name="s5_out",
    )(u, sf.reshape(n_chunk, n_pair * half), sb.reshape(n_chunk, n_pair * half), toep_c, w_of_c, w_ob_c, d_q)


def _glu_body(y_ref, w_ref, b_ref, o_ref):
    y = y_ref[...]
    gy = 0.5 * y * (1.0 + jnp.tanh(math.sqrt(2.0 / math.pi) * (y + 0.044715 * (y * y * y))))
    z = jnp.dot(gy.astype(BF16), w_ref[...], preferred_element_type=F32) + b_ref[...]
    o_ref[...] = (gy * jax.nn.sigmoid(z)).astype(BF16)


def _s5_glu(y, glu_w, glu_b):
    nt, width = y.shape
    tm = _pick(nt, (1280, 640, 256))
    return pl.pallas_call(
        _glu_body,
        grid=(nt // tm,),
        in_specs=[pl.BlockSpec((tm, width), lambda i: (i, 0)),
                  pl.BlockSpec((width, width), lambda i: (0, 0)),
                  pl.BlockSpec((1, width), lambda i: (0, 0))],
        out_specs=pl.BlockSpec((tm, width), lambda i: (i, 0)),
        out_shape=jax.ShapeDtypeStruct((nt, width), BF16),
        compiler_params=_cp(("parallel",), 40),
        name="s5_glu",
    )(y, glu_w, glu_b.reshape(1, width))


def _rope_tables(seq, n_ctx):
    t = jnp.arange(seq)
    row_id = (t // GRID_W).astype(F32)
    col_id = (t % GRID_W).astype(F32)
    n_freq = HEAD_DIM // 4
    inv = ROPE_THETA ** (-jnp.arange(n_freq, dtype=F32) / n_freq)
    ang_r = row_id[:, None] * inv[None, :]
    ang_c = col_id[:, None] * inv[None, :]
    cos_t = jnp.concatenate([jnp.cos(ang_r), jnp.cos(ang_r), jnp.cos(ang_c), jnp.cos(ang_c)], axis=1)
    sin_t = jnp.concatenate([-jnp.sin(ang_r), jnp.sin(ang_r), -jnp.sin(ang_c), jnp.sin(ang_c)], axis=1)
    cos_t = jnp.concatenate([jnp.ones((n_ctx, HEAD_DIM), F32), cos_t], axis=0)
    sin_t = jnp.concatenate([jnp.zeros((n_ctx, HEAD_DIM), F32), sin_t], axis=0)
    return cos_t, sin_t


def kernel(x, c, ctx, c_ctx, w_mod, b_mod, norm_g, ab_w_in, ab_w_out, s5_lam_re, s5_lam_im, s5_log_dt, s5_b_re, s5_b_im, s5_c_re, s5_c_im, s5_d, glu_w, glu_b, attn_sink, ffn_w1, ffn_w3, ffn_w2, c_w_in, c_w_out, q_norm, k_norm, router_w, moe_w1, moe_w3, moe_w2):
    assert x.shape[0] == 1 and ctx.shape[0] == 1
    seq, d = x.shape[1], x.shape[2]
    n_ctx = ctx.shape[1]
    assert n_ctx == ROW_TILE
    depth = w_mod.shape[0]
    s5_width = s5_d.shape[1] * s5_d.shape[2]
    b_heads = attn_sink.shape[1]
    b_kv = (ab_w_in.shape[2] - s5_width - b_heads * HEAD_DIM) // (2 * HEAD_DIM)
    c_heads = c_w_out.shape[1] // HEAD_DIM
    c_kv = (c_w_in.shape[2] - c_heads * HEAD_DIM) // (2 * HEAD_DIM)
    q_scale = HEAD_DIM ** -0.5 * LOG2E

    cond8 = jnp.zeros((8, d), F32).at[0].set(c_ctx).at[1].set(c[0])
    mod4 = _adaln(cond8, w_mod, b_mod).reshape(depth, 8, N_MOD, d)
    cos_t, sin_t = _rope_tables(seq, n_ctx)
    xa, h = _prenorm(x[0], ctx[0], norm_g, mod4)

    def col_vec(pieces):
        return jnp.concatenate([jnp.broadcast_to(jnp.asarray(v, F32), (n,)) for v, n in pieces]).reshape(1, -1)

    bf = lambda w: w.astype(BF16)
    ab_w_in, ab_w_out, c_w_in, c_w_out = bf(ab_w_in), bf(ab_w_out), bf(c_w_in), bf(c_w_out)
    ffn_w1, ffn_w3, ffn_w2 = bf(ffn_w1), bf(ffn_w3), bf(ffn_w2)
    moe_w1, moe_w3, moe_w2 = bf(moe_w1), bf(moe_w3), bf(moe_w2)

    for i in range(depth):
        j = i // 2
        if i % 2 == 0:
            q0 = s5_width
            k0 = q0 + b_heads * HEAD_DIM
            v0 = k0 + b_kv * HEAD_DIM
            n_in = v0 + b_kv * HEAD_DIM
            proj = _inproj(h, ab_w_in, j, cos_t, sin_t,
                           jnp.ones((1, n_in), F32),
                           col_vec([(1.0, q0), (q_scale, k0 - q0), (1.0, n_in - k0)]),
                           q0, v0, False)
            s5w = _s5_weights(s5_lam_re[j], s5_lam_im[j], s5_log_dt[j], s5_b_re[j], s5_b_im[j],
                              s5_c_re[j], s5_c_im[j], s5_d[j])
            a = _s5_glu(_s5(proj[:, :s5_width].astype(F32), n_ctx, s5w), glu_w[j].astype(BF16), glu_b[j])
            o = _window_attention(proj, attn_sink[j], n_ctx, q0, k0, v0, b_kv, b_heads // b_kv)
            blocks = [(a, 0)] + [(o, c) for c in range(o.shape[1] // s5_width)]
            y = _matmul(blocks, ab_w_out, j, s5_width, (1280, 640, 256), 512, BF16, 48, "ab_out")
        else:
            k0 = c_heads * HEAD_DIM
            v0 = k0 + c_kv * HEAD_DIM
            n_in = v0 + c_kv * HEAD_DIM
            proj = _inproj(h, c_w_in, j, cos_t, sin_t,
                           col_vec([(jnp.tile(q_norm[j], c_heads), k0), (jnp.tile(k_norm[j], c_kv), v0 - k0),
                                    (1.0, n_in - v0)]),
                           col_vec([(q_scale, k0), (1.0, n_in - k0)]),
                           0, v0, True)
            o = _flash_attention(proj, n_ctx, 0, k0, v0, c_kv, c_heads // c_kv)
            y = _matmul([(o, 0)], c_w_out, j, o.shape[1], (1280, 640, 256), 512, BF16, 48, "c_out")
        moe_layer = i % 2 == 1
        res = _resid(xa, y, norm_g, mod4, i, 1, 2, i, 2, 3, packed=moe_layer)
        xa, h = res[0], res[1]
        if not moe_layer:
            g = _swiglu_up(h, ffn_w1, ffn_w3, j)
            f = _matmul([(g, 0)], ffn_w2, j, g.shape[1], (1280, 640, 256), 512, BF16, 56, "ffn_down")
            rows = None
        else:
            n_e = router_w.shape[2]
            src, wrow, blk_expert, n_used, rows = _moe_plan(_router(h, router_w[j]), n_e)
            f = _moe_experts(_moe_dispatch(res[2], src), wrow, blk_expert, n_used, moe_w1, moe_w3, moe_w2, j)
        if i + 1 < depth:
            xa, h = _resid(xa, f, norm_g, mod4, i, 3, 5, i + 1, 0, 0, rows=rows)
        else:
            out, = _resid(xa, f, norm_g, mod4, i, 3, 5, rows=rows, row_offset=n_ctx)
    return out[None]
```

```python
import functools
import math

import jax
import jax.numpy as jnp
from jax import lax
from jax.experimental import pallas as pl
from jax.experimental.pallas import tpu as pltpu

F32 = jnp.float32
BF16 = jnp.bfloat16

HEAD_DIM = 128
GRID_W = 64
ROPE_THETA = 10000.0
NORM_EPS = 1e-6
NEG = -1e30
N_MOD = 6
S5_CHUNK = 16
WINDOW = 128
LANES = 128
ROW_TILE = 256
MIB = 1024 * 1024


def _cp(sem, vmem_mib):
    return pltpu.CompilerParams(dimension_semantics=sem, vmem_limit_bytes=vmem_mib * MIB)


def _pick(n, cands):
    for t in cands:
        if n % t == 0:
            return t
    raise ValueError(f"no tile for {n} in {cands}")


def _rms(v):
    return v * lax.rsqrt(jnp.mean(v * v, axis=-1, keepdims=True) + NORM_EPS)


def _adaln_body(cond_ref, w_ref, b_ref, o_ref):
    c = cond_ref[...]
    a = (c * jax.nn.sigmoid(c)).astype(BF16)
    o_ref[...] = jnp.dot(a, w_ref[...].astype(BF16), preferred_element_type=F32) + b_ref[...]


def _adaln(cond8, w_mod, b_mod):
    depth, d, n6 = w_mod.shape
    tn = 512
    return pl.pallas_call(
        _adaln_body,
        grid=(depth, n6 // tn),
        in_specs=[pl.BlockSpec((8, d), lambda l, j: (0, 0)),
                  pl.BlockSpec((None, d, tn), lambda l, j: (l, 0, j)),
                  pl.BlockSpec((None, 1, tn), lambda l, j: (l, 0, j))],
        out_specs=pl.BlockSpec((None, 8, tn), lambda l, j: (l, 0, j)),
        out_shape=jax.ShapeDtypeStruct((depth, 8, n6), F32),
        compiler_params=_cp(("parallel", "parallel"), 40),
        name="adaln",
    )(cond8, w_mod, b_mod.reshape(depth, 1, n6))


def _prenorm_body(x_ref, ctx_ref, g_ref, mod_ref, xa_ref, h_ref):
    i = pl.program_id(0)

    def emit(v):
        xa_ref[...] = v
        n = _rms(v) * g_ref[0:1, :]
        h_ref[...] = (n * (1.0 + mod_ref[1:2, :]) + mod_ref[0:1, :]).astype(BF16)

    @pl.when(i == 0)
    def _():
        emit(ctx_ref[...])

    @pl.when(i > 0)
    def _():
        emit(x_ref[...])


def _prenorm(x2, ctx2, norm_g, mod4):
    seq, d = x2.shape
    nc = ctx2.shape[0]
    nt = seq + nc
    tr = ROW_TILE
    return pl.pallas_call(
        _prenorm_body,
        grid=(nt // tr,),
        in_specs=[pl.BlockSpec((tr, d), lambda i: (jnp.maximum(i - 1, 0), 0)),
                  pl.BlockSpec((tr, d), lambda i: (0, 0)),
                  pl.BlockSpec((None, 4, d), lambda i: (0, 0, 0)),
                  pl.BlockSpec((None, None, N_MOD, d), lambda i: (0, jnp.minimum(i, 1), 0, 0))],
        out_specs=[pl.BlockSpec((tr, d), lambda i: (i, 0)),
                   pl.BlockSpec((tr, d), lambda i: (i, 0))],
        out_shape=[jax.ShapeDtypeStruct((nt, d), F32), jax.ShapeDtypeStruct((nt, d), BF16)],
        compiler_params=_cp(("parallel",), 40),
        name="prenorm",
    )(x2, ctx2, norm_g, mod4)


def _row_gather_start(index_of, src_hbm, dst_ref, sem):
    def issue(r, carry):
        pltpu.make_async_copy(src_hbm.at[pl.ds(index_of(r), 1), :], dst_ref.at[pl.ds(r, 1), :], sem).start()
        return carry

    lax.fori_loop(0, dst_ref.shape[0], issue, 0, unroll=8)


def _row_gather_wait(src_hbm, dst_ref, sem):
    pltpu.make_async_copy(src_hbm.at[pl.ds(0, dst_ref.shape[0]), :], dst_ref, sem).wait()


def _pack_halves(v):
    half = v.shape[1] // 2
    vb = v.astype(BF16).astype(F32)
    hi = lax.bitcast_convert_type(vb[:, :half], jnp.uint32)
    lo = lax.bitcast_convert_type(vb[:, half:], jnp.uint32)
    return (hi & jnp.uint32(0xFFFF0000)) | (lo >> jnp.uint32(16))


def _unpack_halves(u):
    hi = lax.bitcast_convert_type(u & jnp.uint32(0xFFFF0000), F32)
    lo = lax.bitcast_convert_type(u << jnp.uint32(16), F32)
    return jnp.concatenate([hi.astype(BF16), lo.astype(BF16)], axis=1)


def _resid_body(*refs, post_idx, gate_idx, next_g_idx, next_shift_idx, gathered, packed, last):
    refs = list(refs)
    if gathered:
        p_ref = refs.pop(0)
    xa_ref, y_ref, gcur_ref, mcur_ref = refs[:4]
    refs = refs[4:]
    if not last:
        gnext_ref, mnext_ref = refs[:2]
        refs = refs[2:]
    xo_ref = refs.pop(0)
    if gathered:
        buf_ref, sem = refs[-2:]
        _row_gather_start(lambda r: p_ref[0, r], y_ref, buf_ref.at[0], sem)
        _row_gather_start(lambda r: p_ref[1, r], y_ref, buf_ref.at[1], sem)
        _row_gather_wait(y_ref, buf_ref.at[0], sem)
        _row_gather_wait(y_ref, buf_ref.at[1], sem)
        y = buf_ref[0] + buf_ref[1]
    else:
        y = y_ref[...].astype(F32)
    x = xa_ref[...] + mcur_ref[gate_idx:gate_idx + 1, :] * (_rms(y) * gcur_ref[post_idx:post_idx + 1, :])
    xo_ref[...] = x
    if last:
        return
    n = _rms(x) * gnext_ref[next_g_idx:next_g_idx + 1, :]
    h = (n * (1.0 + mnext_ref[next_shift_idx + 1:next_shift_idx + 2, :])
         + mnext_ref[next_shift_idx:next_shift_idx + 1, :])
    refs[0][...] = h.astype(BF16)
    if packed:
        refs[1][...] = _pack_halves(h)


def _resid(xa, y, norm_g, mod4, layer, post_idx, gate_idx, next_layer=None, next_g_idx=0, next_shift_idx=0,
           rows=None, packed=False, row_offset=0):
    nt, d = xa.shape
    tr = ROW_TILE
    off = row_offset // tr
    n_blk = (nt - row_offset) // tr
    last = next_layer is None
    gathered = rows is not None
    row = lambda i: (i + off, 0)
    mod_spec = lambda l: pl.BlockSpec((None, None, N_MOD, d), lambda i: (l, jnp.minimum(i + off, 1), 0, 0))
    in_specs, args = [], []
    if gathered:
        in_specs.append(pl.BlockSpec((None, 2, tr), lambda i: (i + off, 0, 0), memory_space=pltpu.SMEM))
        args.append(rows.reshape(2, nt // tr, tr).transpose(1, 0, 2))
    in_specs += [pl.BlockSpec((tr, d), row),
                 pl.BlockSpec(memory_space=pl.ANY) if gathered else pl.BlockSpec((tr, d), row),
                 pl.BlockSpec((None, 4, d), lambda i: (layer, 0, 0)), mod_spec(layer)]
    args += [xa, y, norm_g, mod4]
    out_specs = [pl.BlockSpec((tr, d), lambda i: (i, 0) if last else (i + off, 0))]
    out_shape = [jax.ShapeDtypeStruct((nt - row_offset if last else nt, d), F32)]
    if not last:
        in_specs += [pl.BlockSpec((None, 4, d), lambda i: (next_layer, 0, 0)), mod_spec(next_layer)]
        args += [norm_g, mod4]
        out_specs.append(pl.BlockSpec((tr, d), row))
        out_shape.append(jax.ShapeDtypeStruct((nt, d), BF16))
        if packed:
            out_specs.append(pl.BlockSpec((tr, d // 2), row))
            out_shape.append(jax.ShapeDtypeStruct((nt, d // 2), jnp.uint32))
    scratch = [pltpu.VMEM((2, tr, d), F32), pltpu.SemaphoreType.DMA(())] if gathered else []
    body = functools.partial(_resid_body, post_idx=post_idx, gate_idx=gate_idx, next_g_idx=next_g_idx,
                             next_shift_idx=next_shift_idx, gathered=gathered, packed=packed, last=last)
    return pl.pallas_call(
        body,
        grid=(n_blk,),
        in_specs=in_specs,
        out_specs=out_specs,
        out_shape=out_shape,
        scratch_shapes=scratch,
        input_output_aliases={} if last else {(1 if gathered else 0): 0},
        compiler_params=_cp(("arbitrary",) if gathered else ("parallel",), 48),
        name="resid",
    )(*args)


def _inproj_body(h_ref, w_ref, cos_ref, sin_ref, gain_ref, scale_ref, o_ref, *, rope_lo, rope_hi, use_norm):
    j = pl.program_id(1)
    acc = jnp.dot(h_ref[...], w_ref[...], preferred_element_type=F32)
    is_rope = jnp.logical_and(j >= rope_lo, j < rope_hi)

    @pl.when(jnp.logical_not(is_rope))
    def _():
        o_ref[...] = acc.astype(BF16)

    @pl.when(is_rope)
    def _():
        c = cos_ref[...]
        s = sin_ref[...]
        lane = lax.broadcasted_iota(jnp.int32, c.shape, 1)
        first = (lane & 32) == 0
        for hh in range(acc.shape[1] // HEAD_DIM):
            sl = slice(hh * HEAD_DIM, (hh + 1) * HEAD_DIM)
            blk = acc[:, sl]
            if use_norm:
                blk = _rms(blk) * gain_ref[:, sl]
            rot = jnp.where(first, pltpu.roll(blk, HEAD_DIM - 32, 1), pltpu.roll(blk, 32, 1))
            o_ref[:, sl] = ((blk * c + rot * s) * scale_ref[:, sl]).astype(BF16)


def _inproj(h, w, layer, cos_t, sin_t, gain, scale, rope_lo, rope_hi, use_norm):
    nt, d = h.shape
    n = w.shape[2]
    tm = _pick(nt, (1280, 640, 256))
    tn = 512
    body = functools.partial(_inproj_body, rope_lo=rope_lo // tn, rope_hi=rope_hi // tn, use_norm=use_norm)
    return pl.pallas_call(
        body,
        grid=(nt // tm, n // tn),
        in_specs=[pl.BlockSpec((tm, d), lambda i, j: (i, 0)),
                  pl.BlockSpec((None, d, tn), lambda i, j: (layer, 0, j)),
                  pl.BlockSpec((tm, HEAD_DIM), lambda i, j: (i, 0)),
                  pl.BlockSpec((tm, HEAD_DIM), lambda i, j: (i, 0)),
                  pl.BlockSpec((1, tn), lambda i, j: (0, j)),
                  pl.BlockSpec((1, tn), lambda i, j: (0, j))],
        out_specs=pl.BlockSpec((tm, tn), lambda i, j: (i, j)),
        out_shape=jax.ShapeDtypeStruct((nt, n), BF16),
        compiler_params=_cp(("parallel", "arbitrary"), 48),
        name="inproj",
    )(h, w, cos_t, sin_t, gain, scale)


def _mm_body(*refs, n_a):
    o_ref = refs[2 * n_a]
    acc = jnp.dot(refs[0][...], refs[n_a][...], preferred_element_type=F32)
    for t in range(1, n_a):
        acc += jnp.dot(refs[t][...], refs[n_a + t][...], preferred_element_type=F32)
    o_ref[...] = acc.astype(o_ref.dtype)


def _matmul(a_blocks, w, layer, kw, tm_cands, tn, out_dtype, vmem_mib, name):
    nt = a_blocks[0][0].shape[0]
    n = w.shape[2]
    tm = _pick(nt, tm_cands)
    in_specs = [pl.BlockSpec((tm, kw), lambda i, j, c=c: (i, c)) for _, c in a_blocks]
    in_specs += [pl.BlockSpec((None, kw, tn), lambda i, j, t=t: (layer, t, j)) for t in range(len(a_blocks))]
    return pl.pallas_call(
        functools.partial(_mm_body, n_a=len(a_blocks)),
        grid=(nt // tm, n // tn),
        in_specs=in_specs,
        out_specs=pl.BlockSpec((tm, tn), lambda i, j: (i, j)),
        out_shape=jax.ShapeDtypeStruct((nt, n), out_dtype),
        compiler_params=_cp(("parallel", "arbitrary"), vmem_mib),
        name=name,
    )(*[a for a, _ in a_blocks], *([w] * len(a_blocks)))


def _swiglu_up_body(h_ref, w1_ref, w3_ref, o_ref):
    h = h_ref[...]
    a = jnp.dot(h, w1_ref[...], preferred_element_type=F32)
    b = jnp.dot(h, w3_ref[...], preferred_element_type=F32)
    o_ref[...] = (a * jax.nn.sigmoid(a) * b).astype(BF16)


def _swiglu_up(h, w1, w3, layer):
    nt, d = h.shape
    f = w1.shape[2]
    tm, tn = _pick(nt, (1280, 640, 256)), 512
    w_spec = pl.BlockSpec((None, d, tn), lambda i, j: (layer, 0, j))
    return pl.pallas_call(
        _swiglu_up_body,
        grid=(nt // tm, f // tn),
        in_specs=[pl.BlockSpec((tm, d), lambda i, j: (i, 0)), w_spec, w_spec],
        out_specs=pl.BlockSpec((tm, tn), lambda i, j: (i, j)),
        out_shape=jax.ShapeDtypeStruct((nt, f), BF16),
        compiler_params=_cp(("parallel", "arbitrary"), 52),
        name="swiglu_up",
    )(h, w1, w3)


def _router_body(h_ref, whi_ref, wlo_ref, g_ref, *, n_experts):
    h = h_ref[...]
    logits = (jnp.dot(h, whi_ref[...], preferred_element_type=F32)
              + jnp.dot(h, wlo_ref[...], preferred_element_type=F32))
    lane = lax.broadcasted_iota(jnp.int32, logits.shape, 1)
    lg = jnp.where(lane < n_experts, logits, NEG)
    v1 = jnp.max(lg, axis=-1, keepdims=True)
    i1 = jnp.min(jnp.where(lg == v1, lane, LANES), axis=-1, keepdims=True)
    lg2 = jnp.where(lane == i1, NEG, lg)
    v2 = jnp.max(lg2, axis=-1, keepdims=True)
    i2 = jnp.min(jnp.where(lg2 == v2, lane, LANES), axis=-1, keepdims=True)
    e = jnp.exp(v2 - v1)
    den = 1.0 + e
    g_ref[...] = (jnp.where(lane == i1, 1.0 / den, 0.0) + jnp.where(lane == i2, e / den, 0.0)
                  + jnp.where(lane == n_experts, i1.astype(F32), 0.0)
                  + jnp.where(lane == n_experts + 1, i2.astype(F32), 0.0))


def _router(h, router_w):
    nt, d = h.shape
    n_e = router_w.shape[1]
    whi = router_w.astype(BF16)
    wlo = (router_w - whi.astype(F32)).astype(BF16)
    pad = ((0, 0), (0, LANES - n_e))
    whi = jnp.pad(whi, pad)
    wlo = jnp.pad(wlo, pad)
    tm = _pick(nt, (640, 256))
    return pl.pallas_call(
        functools.partial(_router_body, n_experts=n_e),
        grid=(nt // tm,),
        in_specs=[pl.BlockSpec((tm, d), lambda i: (i, 0)),
                  pl.BlockSpec((d, LANES), lambda i: (0, 0)),
                  pl.BlockSpec((d, LANES), lambda i: (0, 0))],
        out_specs=pl.BlockSpec((tm, LANES), lambda i: (i, 0)),
        out_shape=jax.ShapeDtypeStruct((nt, LANES), F32),
        compiler_params=_cp(("parallel",), 32),
        name="router",
    )(h, whi, wlo)


MOE_ROWS = 512


def _moe_plan(gates, n_e):
    nt = gates.shape[0]
    tg = MOE_ROWS
    i1 = gates[:, n_e].astype(jnp.int32)
    i2 = gates[:, n_e + 1].astype(jnp.int32)
    w1 = jnp.take_along_axis(gates, i1[:, None], axis=1)[:, 0]
    w2 = jnp.take_along_axis(gates, i2[:, None], axis=1)[:, 0]
    experts = jnp.arange(n_e, dtype=jnp.int32)
    chosen = ((i1[:, None] == experts) | (i2[:, None] == experts)).astype(jnp.int32)
    rank = jnp.cumsum(chosen, axis=0) - chosen
    padded = (jnp.sum(chosen, axis=0) + tg - 1) // tg * tg
    ends = jnp.cumsum(padded)
    pos = (ends - padded)[None, :] + rank
    p1 = jnp.take_along_axis(pos, i1[:, None], axis=1)[:, 0]
    p2 = jnp.take_along_axis(pos, i2[:, None], axis=1)[:, 0]
    n_rows = 2 * nt + n_e * tg
    token = jnp.arange(nt, dtype=jnp.int32)
    pairs = jnp.stack([jnp.concatenate([token, token]).astype(F32), jnp.concatenate([w1, w2])], axis=1)
    table = jnp.zeros((n_rows, 2), F32).at[jnp.concatenate([p1, p2])].set(pairs)
    src, wrow = table[:, 0].astype(jnp.int32), table[:, 1]
    blk_start = jnp.arange(n_rows // tg, dtype=jnp.int32) * tg
    blk_expert = jnp.minimum(jnp.searchsorted(ends, blk_start, side='right'), n_e - 1).astype(jnp.int32)
    n_used = (ends[-1] // tg).astype(jnp.int32).reshape(1)
    return src, wrow, blk_expert, n_used, jnp.stack([p1, p2]).astype(jnp.int32)


def _dispatch_body(src_ref, hp_hbm, o_ref, sem):
    _row_gather_start(lambda r: src_ref[0, r], hp_hbm, o_ref, sem)
    _row_gather_wait(hp_hbm, o_ref, sem)


def _moe_dispatch(hp, src):
    n_rows, tg = src.shape[0], MOE_ROWS
    return pl.pallas_call(
        _dispatch_body,
        grid=(n_rows // tg,),
        in_specs=[pl.BlockSpec((None, 1, tg), lambda b: (b, 0, 0), memory_space=pltpu.SMEM),
                  pl.BlockSpec(memory_space=pl.ANY)],
        out_specs=pl.BlockSpec((tg, hp.shape[1]), lambda b: (b, 0)),
        out_shape=jax.ShapeDtypeStruct((n_rows, hp.shape[1]), jnp.uint32),
        scratch_shapes=[pltpu.SemaphoreType.DMA(())],
        compiler_params=_cp(("arbitrary",), 32),
        name="moe_dispatch",
    )(src.reshape(n_rows // tg, 1, tg), hp)


def _moe_up_body(expert_ref, used_ref, x_ref, w1_ref, w3_ref, gate_ref, o_ref):
    b = pl.program_id(0)

    @pl.when(b < used_ref[0])
    def _():
        x = _unpack_halves(x_ref[...])
        a = jnp.dot(x, w1_ref[...], preferred_element_type=F32)
        c = jnp.dot(x, w3_ref[...], preferred_element_type=F32)
        o_ref[...] = (a * jax.nn.sigmoid(a) * c * gate_ref[...]).astype(BF16)

    @pl.when(b >= used_ref[0])
    def _():
        o_ref[...] = jnp.zeros(o_ref.shape, o_ref.dtype)


def _moe_down_body(expert_ref, used_ref, g_ref, w2_ref, o_ref):
    b = pl.program_id(0)

    @pl.when(b < used_ref[0])
    def _():
        o_ref[...] = jnp.dot(g_ref[...], w2_ref[...], preferred_element_type=F32)

    @pl.when(b >= used_ref[0])
    def _():
        o_ref[...] = jnp.zeros(o_ref.shape, o_ref.dtype)


def _moe_experts(xs, wrow, blk_expert, n_used, w1, w3, w2, layer):
    n_rows, tg = xs.shape[0], MOE_ROWS
    _, n_e, d, f = w1.shape
    row = lambda b, e, u: (b, 0)
    per_expert = lambda b, e, u: (layer, e[b], 0, 0)
    g = pl.pallas_call(
        _moe_up_body,
        grid_spec=pltpu.PrefetchScalarGridSpec(
            num_scalar_prefetch=2, grid=(n_rows // tg,),
            in_specs=[pl.BlockSpec((tg, xs.shape[1]), row), pl.BlockSpec((None, None, d, f), per_expert),
                      pl.BlockSpec((None, None, d, f), per_expert), pl.BlockSpec((tg, 1), row)],
            out_specs=pl.BlockSpec((tg, f), row)),
        out_shape=jax.ShapeDtypeStruct((n_rows, f), BF16),
        compiler_params=_cp(("arbitrary",), 56),
        name="moe_up",
    )(blk_expert, n_used, xs, w1, w3, wrow.reshape(n_rows, 1))
    return pl.pallas_call(
        _moe_down_body,
        grid_spec=pltpu.PrefetchScalarGridSpec(
            num_scalar_prefetch=2, grid=(n_rows // tg,),
            in_specs=[pl.BlockSpec((tg, f), row), pl.BlockSpec((None, None, f, d), per_expert)],
            out_specs=pl.BlockSpec((tg, d), row)),
        out_shape=jax.ShapeDtypeStruct((n_rows, d), F32),
        compiler_params=_cp(("arbitrary",), 48),
        name="moe_down",
    )(blk_expert, n_used, g, w2)


V_ROWS = HEAD_DIM + 16
LOG2E = math.log2(math.e)


def _stack_heads(q, n_grp):
    return jnp.concatenate([q[:, g * HEAD_DIM:(g + 1) * HEAD_DIM] for g in range(n_grp)], axis=0)


def _unstack_store(o_ref, o, n_grp, rows):
    for g in range(n_grp):
        o_ref[:, g * HEAD_DIM:(g + 1) * HEAD_DIM] = o[g * rows:(g + 1) * rows, :].astype(o_ref.dtype)


def _scores_t(k, q):
    return lax.dot_general(k, q, (((1,), (1,)), ((), ())), preferred_element_type=F32)


def _transpose_v(v, n_kv):
    nt = v.shape[0]
    vt = v.reshape(nt, n_kv, HEAD_DIM).transpose(1, 2, 0)
    return jnp.concatenate([vt, jnp.ones((n_kv, V_ROWS - HEAD_DIM, nt), v.dtype)], axis=1)


def _win_body(sink_ref, q_ref, km_ref, k0_ref, kp_ref, kc_ref, vm_ref, v0_ref, vp_ref, vc_ref, o_ref, *,
              n_grp, ctx_blocks, n_lat):
    n = pl.program_id(0)
    blk = q_ref.shape[0]
    n_kv = vc_ref.shape[0]
    width = n_grp * HEAD_DIM
    n_keys = 3 * blk + kc_ref.shape[0]
    n_win = 3 * blk
    shape = (n_keys, n_grp * blk)
    k_off = lax.broadcasted_iota(jnp.int32, shape, 0) - blk
    q_off = lax.broadcasted_iota(jnp.int32, shape, 1) % blk
    kpos = (n - ctx_blocks) * blk + k_off
    in_band = ((jnp.abs(q_off - k_off) <= WINDOW) & (kpos >= 0) & (kpos < n_lat) & (n >= ctx_blocks))
    visible = in_band | (k_off >= n_win - blk)
    for kvh in range(n_kv):
        hd = slice(kvh * HEAD_DIM, (kvh + 1) * HEAD_DIM)
        qs = _stack_heads(q_ref[:, kvh * width:(kvh + 1) * width], n_grp)
        keys = jnp.concatenate([km_ref[:, hd], k0_ref[:, hd], kp_ref[:, hd], kc_ref[:, hd]], axis=0)
        vt = jnp.concatenate([vm_ref[kvh], v0_ref[kvh], vp_ref[kvh], vc_ref[kvh]], axis=1)
        st = jnp.where(visible, _scores_t(keys, qs), NEG)
        sink = jnp.concatenate(
            [jnp.full((1, blk), sink_ref[kvh * n_grp + g] * LOG2E, F32) for g in range(n_grp)], axis=1)
        m = jnp.maximum(jnp.max(st, axis=0, keepdims=True), sink)
        pt = jnp.exp2(st - m).astype(BF16)
        acc = jnp.dot(vt, pt, preferred_element_type=F32)
        den = acc[HEAD_DIM:HEAD_DIM + 1, :] + jnp.exp2(sink - m)
        o = (acc[0:HEAD_DIM, :] / den).T
        for g in range(n_grp):
            col = kvh * width + g * HEAD_DIM
            o_ref[:, col:col + HEAD_DIM] = o[g * blk:(g + 1) * blk, :].astype(o_ref.dtype)


def _window_attention(proj, sink, n_ctx, q_col, k_col, v_col, n_kv, n_grp):
    nt = proj.shape[0]
    blk = WINDOW
    nb = nt // blk
    cb = n_ctx // blk
    kw = n_kv * HEAD_DIM
    qw = n_kv * n_grp * HEAD_DIM
    assert k_col % kw == 0
    kc = k_col // kw
    q_all = proj[:, q_col:q_col + qw]
    vt = _transpose_v(proj[:, v_col:v_col + kw], n_kv)
    lo = lambda n: jnp.maximum(n - 1, cb)
    hi = lambda n: jnp.minimum(n + 1, nb - 1)
    same = lambda n: n
    k_tile = lambda rowf: pl.BlockSpec((blk, kw), lambda n: (rowf(n), kc))
    v_tile = lambda rowf: pl.BlockSpec((n_kv, V_ROWS, blk), lambda n: (0, 0, rowf(n)))
    body = functools.partial(_win_body, n_grp=n_grp, ctx_blocks=cb, n_lat=nt - n_ctx)
    return pl.pallas_call(
        body,
        grid=(nb,),
        in_specs=[pl.BlockSpec(memory_space=pltpu.SMEM),
                  pl.BlockSpec((blk, qw), lambda n: (n, 0)),
                  k_tile(lo), k_tile(same), k_tile(hi),
                  pl.BlockSpec((n_ctx, kw), lambda n: (0, kc)),
                  v_tile(lo), v_tile(same), v_tile(hi),
                  pl.BlockSpec((n_kv, V_ROWS, n_ctx), lambda n: (0, 0, 0))],
        out_specs=pl.BlockSpec((blk, qw), lambda n: (n, 0)),
        out_shape=jax.ShapeDtypeStruct((nt, qw), BF16),
        compiler_params=_cp(("parallel",), 32),
        name="window_attn",
    )(sink, q_all, proj, proj, proj, proj, vt, vt, vt, vt)


def _flash_body(q_ref, k_ref, vt_ref, o_ref, s_ref, acc_ref, *, n_grp, tk, n_chunks, n_ctx):
    qi = pl.program_id(1)
    tq = q_ref.shape[0]
    qs = _stack_heads(q_ref[...], n_grp)
    is_ctx = qi == 0

    def scores(c):
        start = c * tk if isinstance(c, int) else pl.multiple_of(c * tk, tk)
        return _scores_t(k_ref[pl.ds(start, tk), :], qs)

    def absorb(c, slot, m_old):
        m_new = jnp.maximum(m_old, jnp.max(s_ref[slot], axis=0, keepdims=True))
        alpha = jnp.exp2(m_old - m_new)
        pt = jnp.exp2(s_ref[slot] - m_new).astype(BF16)
        acc_ref[...] = alpha * acc_ref[...] + jnp.dot(vt_ref[c], pt, preferred_element_type=F32)
        return m_new

    acc_ref[...] = jnp.zeros(acc_ref.shape, F32)
    st0 = scores(0)
    row = lax.broadcasted_iota(jnp.int32, st0.shape, 0)
    s_ref[0] = jnp.where(row < jnp.where(is_ctx, n_ctx, tk), st0, NEG)
    m = jnp.full((1, qs.shape[0]), NEG, F32)

    def pair(j, m):
        c = 2 * j
        s_ref[1] = scores(c + 1)
        m = absorb(c, 0, m)
        s_ref[0] = scores(c + 2)
        return absorb(c + 1, 1, m)

    n_pairs = (n_chunks - 1) // 2
    m = lax.fori_loop(0, jnp.where(is_ctx, 0, n_pairs), pair, m)
    if (n_chunks - 1) % 2 == 0:
        absorb(jnp.where(is_ctx, 0, n_chunks - 1), 0, m)
    else:
        s_ref[1] = jnp.where(is_ctx, NEG, scores(n_chunks - 1))
        m = absorb(jnp.where(is_ctx, 0, n_chunks - 2), 0, m)
        absorb(n_chunks - 1, 1, m)
    acc = acc_ref[...]
    _unstack_store(o_ref, (acc[0:HEAD_DIM, :] / acc[HEAD_DIM:HEAD_DIM + 1, :]).T, n_grp, tq)


def _flash_attention(proj, n_ctx, q_col, k_col, v_col, n_kv, n_grp):
    nt = proj.shape[0]
    tq = n_ctx
    tk = _pick(nt, (1280, 256))
    n_chunks = nt // tk
    qc, kc = q_col // (n_grp * HEAD_DIM), k_col // HEAD_DIM
    vt = _transpose_v(proj[:, v_col:v_col + n_kv * HEAD_DIM], n_kv)
    vt = vt.reshape(n_kv, V_ROWS, n_chunks, tk).transpose(0, 2, 1, 3)
    body = functools.partial(_flash_body, n_grp=n_grp, tk=tk, n_chunks=n_chunks, n_ctx=n_ctx)
    return pl.pallas_call(
        body,
        grid=(n_kv, nt // tq),
        in_specs=[pl.BlockSpec((tq, n_grp * HEAD_DIM), lambda h, i: (i, qc + h)),
                  pl.BlockSpec((nt, HEAD_DIM), lambda h, i: (0, kc + h)),
                  pl.BlockSpec((None, n_chunks, V_ROWS, tk), lambda h, i: (h, 0, 0, 0))],
        out_specs=pl.BlockSpec((tq, n_grp * HEAD_DIM), lambda h, i: (i, h)),
        out_shape=jax.ShapeDtypeStruct((nt, n_kv * n_grp * HEAD_DIM), BF16),
        scratch_shapes=[pltpu.VMEM((2, tk, n_grp * tq), F32), pltpu.VMEM((V_ROWS, n_grp * tq), F32)],
        compiler_params=_cp(("parallel", "parallel"), 56),
        name="flash_attn",
    )(proj, proj, vt)


def _s5_weights(lam_re, lam_im, log_dt, b_re, b_im, c_re, c_im, d_skip):
    hp = lax.Precision.HIGH
    n_g, n_p = lam_re.shape[1], lam_re.shape[2]
    n_h = b_re.shape[-1]
    ell = S5_CHUNK
    dt = jnp.exp(log_dt)[..., None]
    ar, ai = lam_re * dt, lam_im * dt
    k = jnp.arange(ell + 1, dtype=F32)[:, None, None, None]
    mag = jnp.exp(k * ar[None])
    lp_re, lp_im = mag * jnp.cos(k * ai[None]), mag * jnp.sin(k * ai[None])
    nr, ni = lp_re[1] - 1.0, lp_im[1]
    den = lam_re * lam_re + lam_im * lam_im
    fr, fi = (nr * lam_re + ni * lam_im) / den, (ni * lam_re - nr * lam_im) / den
    bb_re = fr[..., None] * b_re - fi[..., None] * b_im
    bb_im = fr[..., None] * b_im + fi[..., None] * b_re
    lpe_re, lpe_im = lp_re[:, :, :, None, :], lp_im[:, :, :, None, :]
    cl_re = c_re[None] * lpe_re - c_im[None] * lpe_im
    cl_im = c_re[None] * lpe_im + c_im[None] * lpe_re
    m = (jnp.einsum('kdghp,dgpi->kdghi', cl_re, bb_re, precision=hp)
         - jnp.einsum('kdghp,dgpi->kdghi', cl_im, bb_im, precision=hp))
    gl = LANES // n_h
    n_q = n_g // gl
    lane = jnp.arange(LANES)
    put_h = (lane[None, None, :] == (jnp.arange(gl)[:, None, None] * n_h
                                     + jnp.arange(n_h)[None, :, None])).astype(F32)
    put_p = (lane[None, None, :] == (jnp.arange(2)[:, None, None] * n_p
                                     + jnp.arange(n_p)[None, :, None])).astype(F32)
    m_lag = jnp.concatenate([m[ell - 1:0:-1, 1], (m[0, 0] + m[0, 1])[None], m[1:ell, 0]], axis=0)
    m_lag = m_lag.reshape(2 * ell - 1, n_q, gl, n_h, n_h)
    toep_c = jnp.einsum('dqghi,gir,ghl->dqrl', m_lag, put_h, put_h)

    def state_in(direction, powers):
        pr = lp_re[powers, direction].transpose(1, 0, 2)[:, :, None, :]
        pi = lp_im[powers, direction].transpose(1, 0, 2)[:, :, None, :]
        br = bb_re[direction].transpose(0, 2, 1)[:, None, :, :]
        bi = bb_im[direction].transpose(0, 2, 1)[:, None, :, :]
        return pr * br - pi * bi, pr * bi + pi * br

    steps = jnp.arange(ell)
    f_re, f_im = state_in(0, ell - 1 - steps)
    g_re, g_im = state_in(1, steps)
    w_in = jnp.stack([f_re, f_im, g_re, g_im], axis=3)
    w_in = w_in.reshape(n_q, gl // 2, 2, ell, n_h, 4, n_p)
    w_in_c = jnp.einsum('qrsaikp,spl->qarskil', w_in, put_p)
    w_in_c = w_in_c.reshape(n_q, ell, gl, 4, n_h, LANES)

    def state_out(direction, powers):
        wr = cl_re[powers, direction].transpose(1, 3, 0, 2)
        wi = -cl_im[powers, direction].transpose(1, 3, 0, 2)
        w = jnp.stack([wr, wi], axis=1).reshape(n_q, gl, 2, n_p, ell, n_h)
        return jnp.einsum('qgepth,ghl->qgetpl', w, put_h)

    w_of_c = state_out(0, steps + 1)
    w_ob_c = state_out(1, ell - steps)
    lam_l = jnp.stack([lp_re[ell, 0], lp_im[ell, 0], lp_re[ell, 1], lp_im[ell, 1]], axis=0)
    lam_l = lam_l.reshape(4, n_g // 2, 2 * n_p)
    d_q = jnp.broadcast_to(d_skip.reshape(n_q, 1, gl, n_h), (n_q, ell, gl, n_h)).reshape(n_q, 1, ell * LANES)
    return toep_c.astype(BF16), w_in_c.astype(BF16), w_of_c.astype(BF16), w_ob_c.astype(BF16), lam_l, d_q


def _s5_build_toeplitz(toep_ref, t_ref):
    ell = S5_CHUNK
    for a in range(ell):
        for b in range(ell):
            t_ref[a * LANES:(a + 1) * LANES, b * LANES:(b + 1) * LANES] = toep_ref[b - a + ell - 1]


def _s5_build_state_in(w_ref, t_ref):
    _, gl, kinds, n_h, _ = w_ref.shape
    t_ref[...] = jnp.zeros(t_ref.shape, t_ref.dtype)
    for a in range(S5_CHUNK):
        for g in range(gl):
            r0 = (a * gl + g) * n_h
            for k in range(kinds):
                c0 = ((g // 2) * kinds + k) * LANES
                t_ref[r0:r0 + n_h, c0:c0 + LANES] = w_ref[a, g, k]


def _s5_build_state_out(w_ref, t_ref):
    gl, _, _, n_p, _ = w_ref.shape
    for g in range(gl):
        for e in range(2):
            r0 = ((g // 2) * 4 + e * 2 + g % 2) * n_p
            for t in range(S5_CHUNK):
                t_ref[r0:r0 + n_p, t * LANES:(t + 1) * LANES] = w_ref[g, e, t]


def _s5_chunk_rows(u_ref, n_rows):
    return jnp.concatenate([u_ref[pl.ds(t, n_rows, stride=S5_CHUNK), :] for t in range(S5_CHUNK)], axis=1)


def _s5_in_body(u_ref, w_ref, x_ref, win_ref):
    @pl.when(pl.program_id(1) == 0)
    def _():
        _s5_build_state_in(w_ref, win_ref)

    uc = _s5_chunk_rows(u_ref, x_ref.shape[0]).astype(BF16)
    x_ref[...] = jnp.dot(uc, win_ref[...], preferred_element_type=F32)


def _s5_scan_body(xf_ref, xb_ref, lam_ref, sf_ref, sb_ref, st_ref):
    @pl.when(pl.program_id(0) == 0)
    def _():
        st_ref[...] = jnp.zeros(st_ref.shape, F32)

    half = xf_ref.shape[2] // 2
    rows = xf_ref.shape[0]

    def run(x_ref, s_ref, base, order):
        ar, ai = lam_ref[base], lam_ref[base + 1]
        sr, si = st_ref[base], st_ref[base + 1]
        for r in order:
            s_ref[r, :, 0:half] = sr
            s_ref[r, :, half:2 * half] = si
            xr, xi = x_ref[r, :, 0:half], x_ref[r, :, half:2 * half]
            sr, si = ar * sr - ai * si + xr, ar * si + ai * sr + xi
        st_ref[base] = sr
        st_ref[base + 1] = si

    run(xf_ref, sf_ref, 0, range(rows))
    run(xb_ref, sb_ref, 2, range(rows - 1, -1, -1))


def _s5_out_body(u_ref, sf_ref, sb_ref, toep_ref, wof_ref, wob_ref, d_ref, y_ref, t_ref, of_ref, ob_ref):
    @pl.when(pl.program_id(1) == 0)
    def _():
        _s5_build_toeplitz(toep_ref, t_ref)
        _s5_build_state_out(wof_ref, of_ref)
        _s5_build_state_out(wob_ref, ob_ref)

    n_rows = sf_ref.shape[0]
    uc = _s5_chunk_rows(u_ref, n_rows)
    y = jnp.dot(uc.astype(BF16), t_ref[...], preferred_element_type=F32)
    y += jnp.dot(sf_ref[...].astype(BF16), of_ref[...], preferred_element_type=F32)
    y += jnp.dot(sb_ref[...].astype(BF16), ob_ref[...], preferred_element_type=F32)
    y += uc * d_ref[...]
    for t in range(S5_CHUNK):
        y_ref[pl.ds(t, n_rows, stride=S5_CHUNK), :] = y[:, t * LANES:(t + 1) * LANES]


def _s5(u, n_ctx, weights):
    toep_c, w_in_c, w_of_c, w_ob_c, lam_l, d_q = weights
    nt, width = u.shape
    ell = S5_CHUNK
    n_q, _, gl, kinds, _, _ = w_in_c.shape
    kdim = ell * LANES
    n_pair = lam_l.shape[1]
    sdim = kinds * LANES
    sdim_q = (gl // 2) * sdim
    n_chunk = nt // ell
    cb = _pick(n_chunk, (208, 240, 80))
    u_spec = pl.BlockSpec((cb * ell, LANES), lambda q, i: (i, q))

    def per_q(a):
        return pl.BlockSpec((None,) + a.shape[1:], lambda q, i: (q,) + (0,) * (a.ndim - 1))

    x = pl.pallas_call(
        _s5_in_body,
        grid=(n_q, n_chunk // cb),
        in_specs=[u_spec, per_q(w_in_c)],
        out_specs=pl.BlockSpec((cb, sdim_q), lambda q, i: (i, q)),
        out_shape=jax.ShapeDtypeStruct((n_chunk, n_q * sdim_q), F32),
        scratch_shapes=[pltpu.VMEM((kdim, sdim_q), BF16)],
        compiler_params=_cp(("arbitrary", "arbitrary"), 40),
        name="s5_state_in",
    )(u, w_in_c)
    xt = x.reshape(n_chunk, n_pair, sdim)
    rb = n_ctx // ell
    nb = n_chunk // rb
    half = sdim // 2
    bwd = lambda b: (jnp.where(b == 0, 0, nb - b), 0, 0)
    sf, sb = pl.pallas_call(
        _s5_scan_body,
        grid=(nb,),
        in_specs=[pl.BlockSpec((rb, n_pair, half), lambda b: (b, 0, 0)),
                  pl.BlockSpec((rb, n_pair, half), lambda b: (jnp.where(b == 0, 0, nb - b), 0, 1)),
                  pl.BlockSpec((4, n_pair, half // 2), lambda b: (0, 0, 0))],
        out_specs=[pl.BlockSpec((rb, n_pair, half), lambda b: (b, 0, 0)),
                   pl.BlockSpec((rb, n_pair, half), bwd)],
        out_shape=[jax.ShapeDtypeStruct((n_chunk, n_pair, half), F32),
                   jax.ShapeDtypeStruct((n_chunk, n_pair, half), F32)],
        scratch_shapes=[pltpu.VMEM((4, n_pair, half // 2), F32)],
        compiler_params=_cp(("arbitrary",), 32),
        name="s5_scan",
    )(xt, xt, lam_l)
    sdim_h = sdim_q // 2
    s_spec = pl.BlockSpec((cb, sdim_h), lambda q, i: (i, q))
    toep_spec = pl.BlockSpec((toep_c.shape[0], None, LANES, LANES), lambda q, i: (0, q, 0, 0))
    return pl.pallas_call(
        _s5_out_body,
        grid=(n_q, n_chunk // cb),
        in_specs=[u_spec, s_spec, s_spec, toep_spec, per_q(w_of_c), per_q(w_ob_c), per_q(d_q)],
        out_specs=u_spec,
        out_shape=jax.ShapeDtypeStruct((nt, width), F32),
        scratch_shapes=[pltpu.VMEM((kdim, kdim), BF16), pltpu.VMEM((sdim_h, kdim), BF16),
                        pltpu.VMEM((sdim_h, kdim), BF16)],
        compiler_params=_cp(("arbitrary", "arbitrary"), 56),
        name="s5_out",
    )(u, sf.reshape(n_chunk, n_pair * half), sb.reshape(n_chunk, n_pair * half), toep_c, w_of_c, w_ob_c, d_q)


def _glu_body(y_ref, w_ref, b_ref, o_ref):
    y = y_ref[...]
    gy = 0.5 * y * (1.0 + jnp.tanh(math.sqrt(2.0 / math.pi) * (y + 0.044715 * (y * y * y))))
    z = jnp.dot(gy.astype(BF16), w_ref[...], preferred_element_type=F32) + b_ref[...]
    o_ref[...] = (gy * jax.nn.sigmoid(z)).astype(BF16)


def _s5_glu(y, glu_w, glu_b):
    nt, width = y.shape
    tm = _pick(nt, (1280, 640, 256))
    return pl.pallas_call(
        _glu_body,
        grid=(nt // tm,),
        in_specs=[pl.BlockSpec((tm, width), lambda i: (i, 0)),
                  pl.BlockSpec((width, width), lambda i: (0, 0)),
                  pl.BlockSpec((1, width), lambda i: (0, 0))],
        out_specs=pl.BlockSpec((tm, width), lambda i: (i, 0)),
        out_shape=jax.ShapeDtypeStruct((nt, width), BF16),
        compiler_params=_cp(("parallel",), 40),
        name="s5_glu",
    )(y, glu_w, glu_b.reshape(1, width))


def _rope_tables(seq, n_ctx):
    t = jnp.arange(seq)
    row_id = (t // GRID_W).astype(F32)
    col_id = (t % GRID_W).astype(F32)
    n_freq = HEAD_DIM // 4
    inv = ROPE_THETA ** (-jnp.arange(n_freq, dtype=F32) / n_freq)
    ang_r = row_id[:, None] * inv[None, :]
    ang_c = col_id[:, None] * inv[None, :]
    cos_t = jnp.concatenate([jnp.cos(ang_r), jnp.cos(ang_r), jnp.cos(ang_c), jnp.cos(ang_c)], axis=1)
    sin_t = jnp.concatenate([-jnp.sin(ang_r), jnp.sin(ang_r), -jnp.sin(ang_c), jnp.sin(ang_c)], axis=1)
    cos_t = jnp.concatenate([jnp.ones((n_ctx, HEAD_DIM), F32), cos_t], axis=0)
    sin_t = jnp.concatenate([jnp.zeros((n_ctx, HEAD_DIM), F32), sin_t], axis=0)
    return cos_t, sin_t


def kernel(x, c, ctx, c_ctx, w_mod, b_mod, norm_g, ab_w_in, ab_w_out, s5_lam_re, s5_lam_im, s5_log_dt, s5_b_re, s5_b_im, s5_c_re, s5_c_im, s5_d, glu_w, glu_b, attn_sink, ffn_w1, ffn_w3, ffn_w2, c_w_in, c_w_out, q_norm, k_norm, router_w, moe_w1, moe_w3, moe_w2):
    assert x.shape[0] == 1 and ctx.shape[0] == 1
    seq, d = x.shape[1], x.shape[2]
    n_ctx = ctx.shape[1]
    assert n_ctx == ROW_TILE
    depth = w_mod.shape[0]
    s5_width = s5_d.shape[1] * s5_d.shape[2]
    b_heads = attn_sink.shape[1]
    b_kv = (ab_w_in.shape[2] - s5_width - b_heads * HEAD_DIM) // (2 * HEAD_DIM)
    c_heads = c_w_out.shape[1] // HEAD_DIM
    c_kv = (c_w_in.shape[2] - c_heads * HEAD_DIM) // (2 * HEAD_DIM)
    q_scale = HEAD_DIM ** -0.5 * LOG2E

    cond8 = jnp.zeros((8, d), F32).at[0].set(c_ctx).at[1].set(c[0])
    mod4 = _adaln(cond8, w_mod, b_mod).reshape(depth, 8, N_MOD, d)
    cos_t, sin_t = _rope_tables(seq, n_ctx)
    xa, h = _prenorm(x[0], ctx[0], norm_g, mod4)

    def col_vec(pieces):
        return jnp.concatenate([jnp.broadcast_to(jnp.asarray(v, F32), (n,)) for v, n in pieces]).reshape(1, -1)

    bf = lambda w: w.astype(BF16)
    ab_w_in, ab_w_out, c_w_in, c_w_out = bf(ab_w_in), bf(ab_w_out), bf(c_w_in), bf(c_w_out)
    ffn_w1, ffn_w3, ffn_w2 = bf(ffn_w1), bf(ffn_w3), bf(ffn_w2)
    moe_w1, moe_w3, moe_w2 = bf(moe_w1), bf(moe_w3), bf(moe_w2)

    for i in range(depth):
        j = i // 2
        if i % 2 == 0:
            q0 = s5_width
            k0 = q0 + b_heads * HEAD_DIM
            v0 = k0 + b_kv * HEAD_DIM
            n_in = v0 + b_kv * HEAD_DIM
            proj = _inproj(h, ab_w_in, j, cos_t, sin_t,
                           jnp.ones((1, n_in), F32),
                           col_vec([(1.0, q0), (q_scale, k0 - q0), (1.0, n_in - k0)]),
                           q0, v0, False)
            s5w = _s5_weights(s5_lam_re[j], s5_lam_im[j], s5_log_dt[j], s5_b_re[j], s5_b_im[j],
                              s5_c_re[j], s5_c_im[j], s5_d[j])
            a = _s5_glu(_s5(proj[:, :s5_width].astype(F32), n_ctx, s5w), glu_w[j].astype(BF16), glu_b[j])
            o = _window_attention(proj, attn_sink[j], n_ctx, q0, k0, v0, b_kv, b_heads // b_kv)
            blocks = [(a, 0)] + [(o, c) for c in range(o.shape[1] // s5_width)]
            y = _matmul(blocks, ab_w_out, j, s5_width, (1280, 640, 256), 512, BF16, 48, "ab_out")
        else:
            k0 = c_heads * HEAD_DIM
            v0 = k0 + c_kv * HEAD_DIM
            n_in = v0 + c_kv * HEAD_DIM
            proj = _inproj(h, c_w_in, j, cos_t, sin_t,
                           col_vec([(jnp.tile(q_norm[j], c_heads), k0), (jnp.tile(k_norm[j], c_kv), v0 - k0),
                                    (1.0, n_in - v0)]),
                           col_vec([(q_scale, k0), (1.0, n_in - k0)]),
                           0, v0, True)
            o = _flash_attention(proj, n_ctx, 0, k0, v0, c_kv, c_heads // c_kv)
            y = _matmul([(o, 0)], c_w_out, j, o.shape[1], (1280, 640, 256), 512, BF16, 48, "c_out")
        moe_layer = i % 2 == 1
        res = _resid(xa, y, norm_g, mod4, i, 1, 2, i, 2, 3, packed=moe_layer)
        xa, h = res[0], res[1]
        if not moe_layer:
            g = _swiglu_up(h, ffn_w1, ffn_w3, j)
            f = _matmul([(g, 0)], ffn_w2, j, g.shape[1], (1280, 640, 256), 512, BF16, 56, "ffn_down")
            rows = None
        else:
            n_e = router_w.shape[2]
            src, wrow, blk_expert, n_used, rows = _moe_plan(_router(h, router_w[j]), n_e)
            f = _moe_experts(_moe_dispatch(res[2], src), wrow, blk_expert, n_used, moe_w1, moe_w3, moe_w2, j)
        if i + 1 < depth:
            xa, h = _resid(xa, f, norm_g, mod4, i, 3, 5, i + 1, 0, 0, rows=rows)
        else:
            out, = _resid(xa, f, norm_g, mod4, i, 3, 5, rows=rows, row_offset=n_ctx)
    return out[None]
```
